```python
import math
import jax, jax.numpy as jnp
from jax import lax
import numpy as np

D_MODEL = 2048
BATCH = 2
SEQ = 4096
DEPTH = 1
DEC_BATCH = 32
DEC_SEQ = 4
PAST_LEN = 16384
PAGE_SIZE = 128

MIX_DIM = D_MODEL
ATT_DIM = MIX_DIM // 2
CONV_DIM = MIX_DIM - ATT_DIM
QK_DIM = 64
V_DIM = 2 * QK_DIM
N_ATT_HEADS = ATT_DIM // V_DIM
CONV_K = 3
IN_DIM = 3 * ATT_DIM + 3 * CONV_DIM
ROPE_THETA = 10000.0
SUBLN_EPS = 1e-5
LN_EPS = 1e-5
Q_BLOCK = 128
N_EXPERTS = 32
TOP_K = 4
D_FF = D_MODEL
SWIGLU_ALPHA = 1.702
SWIGLU_LIMIT = 7.0
MOE_BLOCK = 128
DEEPNORM_ALPHA = (2.0 * DEPTH) ** 0.25
DEEPNORM_BETA = (8.0 * DEPTH) ** -0.25

kernel_name = 'hybrid_diffattn_shortconv_moe_step'


def layer_norm(x, g, b):
    xf = x.astype(jnp.float32)
    mu = jnp.mean(xf, axis=-1, keepdims=True)
    var = jnp.mean(jnp.square(xf - mu), axis=-1, keepdims=True)
    return ((xf - mu) * lax.rsqrt(var + LN_EPS) * g + b).astype(x.dtype)


def rms_norm(x, g):
    xf = x.astype(jnp.float32)
    return xf * lax.rsqrt(jnp.mean(jnp.square(xf), axis=-1, keepdims=True) + SUBLN_EPS) * g


def rope(x, pos):
    half = QK_DIM // 2
    inv = jnp.exp(-math.log(ROPE_THETA) * jnp.arange(half, dtype=jnp.float32) * (2.0 / QK_DIM))
    ang = pos.astype(jnp.float32)[:, None] * inv[None, :]
    cos = jnp.cos(ang)[:, None, None, :]
    sin = jnp.sin(ang)[:, None, None, :]
    xf = x.astype(jnp.float32)
    x1, x2 = xf[..., :half], xf[..., half:]
    return jnp.concatenate([x1 * cos - x2 * sin, x2 * cos + x1 * sin], axis=-1).astype(x.dtype)


def diff_core(q, k, v, mask, lam):
    s = jnp.einsum('bqhmd,bkhmd->bhmqk', q, k, preferred_element_type=jnp.float32) * (QK_DIM ** -0.5)
    s = jnp.where(mask, s, -jnp.inf)
    a = jax.nn.softmax(s, axis=-1)
    w = a[:, :, 0] - lam * a[:, :, 1]
    return jnp.einsum('bhqk,bkhv->bqhv', w, v.astype(jnp.float32))


def prompt_attention(q, k, v, lam):
    b, s = q.shape[0], q.shape[1]
    nblk = s // Q_BLOCK
    qb = jnp.moveaxis(q.reshape(b, nblk, Q_BLOCK, N_ATT_HEADS, 2, QK_DIM), 1, 0)
    kpos = jnp.arange(s)

    def one_block(args):
        qi, i = args
        qpos = i * Q_BLOCK + jnp.arange(Q_BLOCK)
        mask = kpos[None, :] <= qpos[:, None]
        return diff_core(qi, k, v, mask, lam)

    out = lax.map(one_block, (qb, jnp.arange(nblk)))
    return jnp.moveaxis(out, 0, 1).reshape(b, s, N_ATT_HEADS, V_DIM)


def sample_attention(q, k_new, v_new, cache_k, cache_v, layer, page_table, lam):
    t = q.shape[1]
    n_past = page_table.shape[1] * PAGE_SIZE
    mask = jnp.concatenate([jnp.ones((t, n_past), dtype=bool), jnp.tril(jnp.ones((t, t), dtype=bool))], axis=1)

    def one_seq(args):
        qi, kn, vn, pt = args
        kp = cache_k[layer, pt].reshape(n_past, N_ATT_HEADS, 2, QK_DIM).astype(kn.dtype)
        vp = cache_v[layer, pt].reshape(n_past, N_ATT_HEADS, V_DIM).astype(vn.dtype)
        kk = jnp.concatenate([kp, kn], axis=0)[None]
        vv = jnp.concatenate([vp, vn], axis=0)[None]
        return diff_core(qi[None], kk, vv, mask, lam)[0]

    return lax.map(one_seq, (q, k_new, v_new, page_table))


def short_conv(u, state, w):
    s = u.shape[1]
    full = jnp.concatenate([state.astype(u.dtype), u], axis=1)
    y = full[:, 0:s] * w[0]
    for j in range(1, CONV_K):
        y = y + full[:, j:j + s] * w[j]
    return y, full[:, full.shape[1] - (CONV_K - 1):]


def mixer_project(x, w_in, pos):
    b, s, _ = x.shape
    p = x @ w_in
    q = rope(p[..., :ATT_DIM].reshape(b, s, N_ATT_HEADS, 2, QK_DIM), pos)
    k = rope(p[..., ATT_DIM:2 * ATT_DIM].reshape(b, s, N_ATT_HEADS, 2, QK_DIM), pos)
    v = p[..., 2 * ATT_DIM:3 * ATT_DIM].reshape(b, s, N_ATT_HEADS, V_DIM)
    o = 3 * ATT_DIM
    h = p[..., o:o + CONV_DIM]
    gate_b = p[..., o + CONV_DIM:o + 2 * CONV_DIM]
    gate_c = p[..., o + 2 * CONV_DIM:o + 3 * CONV_DIM]
    return q, k, v, gate_c * h, gate_b


def moe_ffn(x, w_router, b_router, w_gate, b_gate, w_up, b_up, w_down, b_down):
    b, s, d = x.shape
    xt = x.reshape(-1, d)
    t = xt.shape[0]
    logits = jnp.matmul(xt, w_router, preferred_element_type=jnp.float32) + b_router.astype(jnp.float32)
    top_val, top_idx = lax.top_k(logits, TOP_K)
    gates = jax.nn.softmax(top_val, axis=-1)
    e_flat = top_idx.reshape(-1)
    tok_flat = jnp.repeat(jnp.arange(t, dtype=jnp.int32), TOP_K)
    g_flat = gates.reshape(-1)
    order = jnp.argsort(e_flat)
    e_sorted = e_flat[order]
    counts = jnp.bincount(e_flat, length=N_EXPERTS)
    start = jnp.cumsum(counts) - counts
    padded = (counts + MOE_BLOCK - 1) // MOE_BLOCK * MOE_BLOCK
    pend = jnp.cumsum(padded)
    pstart = pend - padded
    dest = pstart[e_sorted] + (jnp.arange(t * TOP_K) - start[e_sorted])
    n_blocks = -(-(t * TOP_K) // MOE_BLOCK) + N_EXPERTS
    n_rows = n_blocks * MOE_BLOCK
    row_tok = jnp.zeros((n_rows,), jnp.int32).at[dest].set(tok_flat[order])
    row_gate = jnp.zeros((n_rows,), jnp.float32).at[dest].set(g_flat[order])
    blk_exp = jnp.minimum(jnp.searchsorted(pend, jnp.arange(n_blocks) * MOE_BLOCK, side='right'), N_EXPERTS - 1)
    xr = xt[row_tok].reshape(n_blocks, MOE_BLOCK, d)

    def expert_block(args):
        xb, e = args
        g = xb @ w_gate[e] + b_gate[e]
        u = xb @ w_up[e] + b_up[e]
        g = jnp.minimum(g, SWIGLU_LIMIT)
        u = jnp.clip(u, -SWIGLU_LIMIT, SWIGLU_LIMIT)
        hdn = g * jax.nn.sigmoid(SWIGLU_ALPHA * g) * (u + 1.0)
        return hdn @ w_down[e] + b_down[e]

    yr = lax.map(expert_block, (xr, blk_exp)).reshape(n_rows, d)
    out = jnp.zeros((t, d), jnp.float32).at[row_tok].add(yr.astype(jnp.float32) * row_gate[:, None])
    return out.astype(x.dtype).reshape(b, s, d)


def layer_output(x, att, conv_y, gate_b, lam_init, subln_g, w_out, ln1_g, ln1_b, moe_params, ln2_g, ln2_b):
    b, s, _ = x.shape
    att = rms_norm(att, subln_g) * (1.0 - lam_init)
    mixed = jnp.concatenate([att.reshape(b, s, ATT_DIM).astype(x.dtype), (gate_b * conv_y).astype(x.dtype)], axis=-1)
    h = layer_norm(DEEPNORM_ALPHA * x + mixed @ w_out, ln1_g, ln1_b)
    return layer_norm(DEEPNORM_ALPHA * h + moe_ffn(h, *moe_params), ln2_g, ln2_b)


def setup_inputs(seed: int = 0) -> dict:
    key = jax.random.key(seed)
    ks = jax.random.split(key, 24)
    f32 = jnp.float32
    n_pages = PAST_LEN // PAGE_SIZE
    n_used = DEC_BATCH * n_pages
    n_pool = n_used + (n_used + 3) // 4

    def nrm(k, shape, scale):
        return jax.random.normal(k, shape, f32) * scale

    return {
        'x_prompt': nrm(ks[0], (BATCH, SEQ, D_MODEL), 1.0),
        'x_sample': nrm(ks[1], (DEC_BATCH, DEC_SEQ, D_MODEL), 1.0),
        'cache_k': nrm(ks[2], (DEPTH, n_pool, PAGE_SIZE, N_ATT_HEADS, 2 * QK_DIM), 1.0),
        'cache_v': nrm(ks[3], (DEPTH, n_pool, PAGE_SIZE, N_ATT_HEADS, V_DIM), 1.0),
        'state_conv': nrm(ks[4], (DEPTH, DEC_BATCH, CONV_K - 1, CONV_DIM), 1.0),
        'page_table': jax.random.permutation(ks[5], n_pool)[:n_used].reshape(DEC_BATCH, n_pages).astype(jnp.int32),
        'w_in': nrm(ks[6], (DEPTH, D_MODEL, IN_DIM), D_MODEL ** -0.5),
        'conv_w': nrm(ks[7], (DEPTH, CONV_K, CONV_DIM), CONV_K ** -0.5),
        'lambda_q': nrm(ks[8], (DEPTH, 2, QK_DIM), 0.1),
        'lambda_k': nrm(ks[9], (DEPTH, 2, QK_DIM), 0.1),
        'subln_g': 1.0 + nrm(ks[10], (DEPTH, V_DIM), 0.02),
        'w_out': nrm(ks[11], (DEPTH, MIX_DIM, D_MODEL), MIX_DIM ** -0.5 * DEEPNORM_BETA),
        'ln1_g': 1.0 + nrm(ks[12], (DEPTH, D_MODEL), 0.02),
        'ln1_b': nrm(ks[13], (DEPTH, D_MODEL), 0.02),
        'w_router': nrm(ks[14], (DEPTH, D_MODEL, N_EXPERTS), D_MODEL ** -0.5),
        'b_router': nrm(ks[15], (DEPTH, N_EXPERTS), 0.01),
        'w_gate': nrm(ks[16], (DEPTH, N_EXPERTS, D_MODEL, D_FF), D_MODEL ** -0.5),
        'b_gate': nrm(ks[17], (DEPTH, N_EXPERTS, D_FF), 0.02),
        'w_up': nrm(ks[18], (DEPTH, N_EXPERTS, D_MODEL, D_FF), D_MODEL ** -0.5),
        'b_up': nrm(ks[19], (DEPTH, N_EXPERTS, D_FF), 0.02),
        'w_down': nrm(ks[20], (DEPTH, N_EXPERTS, D_FF, D_MODEL), D_FF ** -0.5 * DEEPNORM_BETA),
        'b_down': nrm(ks[21], (DEPTH, N_EXPERTS, D_MODEL), 0.02),
        'ln2_g': 1.0 + nrm(ks[22], (DEPTH, D_MODEL), 0.02),
        'ln2_b': nrm(ks[23], (DEPTH, D_MODEL), 0.02),
    }


def reference(x_prompt, x_sample, cache_k, cache_v, state_conv, page_table, w_in, conv_w, lambda_q, lambda_k, subln_g, w_out, ln1_g, ln1_b, w_router, b_router, w_gate, b_gate, w_up, b_up, w_down, b_down, ln2_g, ln2_b):
    bp, sp = x_prompt.shape[0], x_prompt.shape[1]
    bs, ts = x_sample.shape[0], x_sample.shape[1]
    n_past = page_table.shape[1] * PAGE_SIZE
    pos_p = jnp.arange(sp)
    pos_s = n_past + jnp.arange(ts)
    xp, xs = x_prompt, x_sample
    kp_l, vp_l, cp_l, ks_l, vs_l, cs_l = [], [], [], [], [], []
    for l in range(DEPTH):
        lam_init = 0.8 - 0.6 * math.exp(-0.3 * l)
        lq = lambda_q[l].astype(jnp.float32)
        lk = lambda_k[l].astype(jnp.float32)
        lam = jnp.exp(jnp.sum(lq[0] * lk[0])) - jnp.exp(jnp.sum(lq[1] * lk[1])) + lam_init
        moe_params = (w_router[l], b_router[l], w_gate[l], b_gate[l], w_up[l], b_up[l], w_down[l], b_down[l])
        q, k, v, u, gb = mixer_project(xp, w_in[l], pos_p)
        att = prompt_attention(q, k, v, lam)
        cy, cst = short_conv(u, jnp.zeros((bp, CONV_K - 1, CONV_DIM), u.dtype), conv_w[l])
        kp_l.append(k.reshape(bp, sp, N_ATT_HEADS, 2 * QK_DIM))
        vp_l.append(v)
        cp_l.append(cst)
        xp = layer_output(xp, att, cy, gb, lam_init, subln_g[l], w_out[l], ln1_g[l], ln1_b[l], moe_params, ln2_g[l], ln2_b[l])
        q, k, v, u, gb = mixer_project(xs, w_in[l], pos_s)
        att = sample_attention(q, k, v, cache_k, cache_v, l, page_table, lam)
        cy, cst = short_conv(u, state_conv[l], conv_w[l])
        ks_l.append(k.reshape(bs, ts, N_ATT_HEADS, 2 * QK_DIM))
        vs_l.append(v)
        cs_l.append(cst)
        xs = layer_output(xs, att, cy, gb, lam_init, subln_g[l], w_out[l], ln1_g[l], ln1_b[l], moe_params, ln2_g[l], ln2_b[l])
    k_prompt = jnp.stack(kp_l)
    v_prompt = jnp.stack(vp_l)
    conv_prompt = jnp.stack(cp_l)
    k_sample = jnp.stack(ks_l)
    v_sample = jnp.stack(vs_l)
    conv_sample = jnp.stack(cs_l)
    return (xp, xs, k_prompt, v_prompt, conv_prompt, k_sample, v_sample, conv_sample)
```

```python
import functools
import math

import jax
import jax.numpy as jnp
from jax import lax
from jax.experimental import pallas as pl
from jax.experimental.pallas import tpu as pltpu

QK_DIM = 64
V_DIM = 2 * QK_DIM
CONV_K = 3
PAGE_SIZE = 128
ROPE_THETA = 10000.0
SUBLN_EPS = 1e-5
LN_EPS = 1e-5
TOP_K = 4
SWIGLU_ALPHA = 1.702
SWIGLU_LIMIT = 7.0

LANES = 128
NEG_BIG = -1e30

VMEM_LIMIT = 56 * 1024 * 1024

MOE_SB = 256
MOE_NSUB = 8
MOE_F = 256
DISPATCH_TD = 128
SAMPLE_PAGES = 8


def _cparams(sem):
    return pltpu.CompilerParams(dimension_semantics=sem, vmem_limit_bytes=VMEM_LIMIT)


def _tile(n, pref):
    t = min(n, pref)
    while n % t:
        t //= 2
    return t


def _rope_chunks(a, cos, sin_signed):
    lane = lax.broadcasted_iota(jnp.int32, (a.shape[0], LANES), 1)
    first = (lane % QK_DIM) < (QK_DIM // 2)
    outs = []
    for c in range(a.shape[1] // LANES):
        ac = a[:, c * LANES:(c + 1) * LANES]
        partner = jnp.where(first, pltpu.roll(ac, LANES - QK_DIM // 2, 1), pltpu.roll(ac, QK_DIM // 2, 1))
        outs.append(ac * cos + partner * sin_signed)
    return outs


def _qkv_kernel(x_ref, w_ref, cos_ref, sin_ref, qb_ref, kf_ref, kb_ref, vf_ref, vb_ref, *, nsec):
    j = pl.program_id(1)
    acc = jnp.dot(x_ref[...], w_ref[...], preferred_element_type=jnp.float32)

    @pl.when(j < nsec)
    def _q():
        for c, o in enumerate(_rope_chunks(acc, cos_ref[...], sin_ref[...])):
            qb_ref[:, c * LANES:(c + 1) * LANES] = (o * (QK_DIM ** -0.5)).astype(qb_ref.dtype)

    @pl.when((j >= nsec) & (j < 2 * nsec))
    def _k():
        for c, o in enumerate(_rope_chunks(acc, cos_ref[...], sin_ref[...])):
            kf_ref[:, c * LANES:(c + 1) * LANES] = o
            kb_ref[:, c * LANES:(c + 1) * LANES] = o.astype(kb_ref.dtype)

    @pl.when(j >= 2 * nsec)
    def _v():
        vf_ref[...] = acc
        vb_ref[...] = acc.astype(vb_ref.dtype)


def _qkv_proj(x_bf, w_bf, cos, sin_signed, att_dim):
    rows, d = x_bf.shape
    tm = _tile(rows, 1024)
    tn = _tile(att_dim, 512)
    nsec = att_dim // tn
    sec = lambda s: (lambda i, j: (i, jnp.clip(j - s * nsec, 0, nsec - 1)))
    f32 = jax.ShapeDtypeStruct((rows, att_dim), jnp.float32)
    b16 = jax.ShapeDtypeStruct((rows, att_dim), jnp.bfloat16)
    return pl.pallas_call(
        functools.partial(_qkv_kernel, nsec=nsec),
        grid=(rows // tm, 3 * nsec),
        in_specs=[
            pl.BlockSpec((tm, d), lambda i, j: (i, 0)),
            pl.BlockSpec((d, tn), lambda i, j: (0, j)),
            pl.BlockSpec((tm, LANES), lambda i, j: (i, 0)),
            pl.BlockSpec((tm, LANES), lambda i, j: (i, 0)),
        ],
        out_specs=[
            pl.BlockSpec((tm, tn), sec(0)),
            pl.BlockSpec((tm, tn), sec(1)),
            pl.BlockSpec((tm, tn), sec(1)),
            pl.BlockSpec((tm, tn), sec(2)),
            pl.BlockSpec((tm, tn), sec(2)),
        ],
        out_shape=[b16, f32, b16, f32, b16],
        compiler_params=_cparams(("parallel", "arbitrary")),
        name="qkv_proj",
    )(x_bf, w_bf, cos, sin_signed)


def _convproj_kernel(x_ref, wh_ref, wb_ref, wc_ref, u_ref, gb_ref):
    x = x_ref[...]
    h = jnp.dot(x, wh_ref[...], preferred_element_type=jnp.float32)
    gc = jnp.dot(x, wc_ref[...], preferred_element_type=jnp.float32)
    u_ref[...] = gc * h
    gb_ref[...] = jnp.dot(x, wb_ref[...], preferred_element_type=jnp.float32)


def _conv_proj(x_bf, w_bf, att_dim, conv_dim):
    rows, d = x_bf.shape
    tm = _tile(rows, 1024)
    tn = _tile(conv_dim, 512)
    base = 3 * att_dim // tn
    nsec = conv_dim // tn
    wspec = lambda s: pl.BlockSpec((d, tn), lambda i, j: (0, base + s * nsec + j))
    f32 = jax.ShapeDtypeStruct((rows, conv_dim), jnp.float32)
    return pl.pallas_call(
        _convproj_kernel,
        grid=(rows // tm, nsec),
        in_specs=[pl.BlockSpec((tm, d), lambda i, j: (i, 0)), wspec(0), wspec(1), wspec(2)],
        out_specs=[pl.BlockSpec((tm, tn), lambda i, j: (i, j))] * 2,
        out_shape=[f32, f32],
        compiler_params=_cparams(("parallel", "arbitrary")),
        name="conv_proj",
    )(x_bf, w_bf, w_bf, w_bf)


def _conv_gate_kernel(s0_ref, s1_ref, s2_ref, gb_ref, w_ref, o_ref):
    w = w_ref[...]
    y = s0_ref[...] * w[0:1] + s1_ref[...] * w[1:2] + s2_ref[...] * w[2:3]
    o_ref[...] = (gb_ref[...] * y).astype(o_ref.dtype)


def _conv_gate(s0, s1, s2, gb, conv_w):
    rows, c = gb.shape
    tm = _tile(rows, 512)
    blk = pl.BlockSpec((tm, c), lambda i: (i, 0))
    return pl.pallas_call(
        _conv_gate_kernel,
        grid=(rows // tm,),
        in_specs=[blk, blk, blk, blk, pl.BlockSpec((CONV_K, c), lambda i: (0, 0))],
        out_specs=blk,
        out_shape=jax.ShapeDtypeStruct((rows, c), jnp.bfloat16),
        compiler_params=_cparams(("parallel",)),
        name="conv_gate",
    )(s0, s1, s2, gb, conv_w)


def _lambda_full(lq_ref, lk_ref, lam_init):
    prod = lq_ref[...] * lk_ref[...]
    e = jnp.exp(jnp.sum(prod, axis=-1, keepdims=True))
    return e[0:1] - e[1:2] + lam_init


def _subln(o, g, lam_init):
    ms = jnp.mean(o * o, axis=-1, keepdims=True)
    return o * lax.rsqrt(ms + SUBLN_EPS) * g * (1.0 - lam_init)


def _flash_kernel(q_ref, k_ref, v_ref, lq_ref, lk_ref, g_ref, o_ref, qq_ref, m_ref, l_ref, acc_ref,
                  *, tq, lam_init):
    iq = pl.program_id(2)
    ik = pl.program_id(3)

    @pl.when(ik == 0)
    def _init():
        q = q_ref[...]
        lane = lax.broadcasted_iota(jnp.int32, q.shape, 1)
        zero = jnp.zeros_like(q)
        qq_ref[0:tq] = jnp.where(lane < QK_DIM, q, zero)
        qq_ref[tq:2 * tq] = jnp.where(lane >= QK_DIM, q, zero)
        m_ref[...] = jnp.full(m_ref.shape, NEG_BIG, jnp.float32)
        l_ref[...] = jnp.zeros(l_ref.shape, jnp.float32)
        acc_ref[...] = jnp.zeros(acc_ref.shape, jnp.float32)

    def step(masked):
        s = lax.dot_general(qq_ref[...], k_ref[...], (((1,), (1,)), ((), ())),
                            preferred_element_type=jnp.float32)
        if masked:
            row = lax.broadcasted_iota(jnp.int32, s.shape, 0)
            col = lax.broadcasted_iota(jnp.int32, s.shape, 1)
            row = jnp.where(row >= tq, row - tq, row)
            s = jnp.where(row >= col, s, NEG_BIG)
        m_prev = m_ref[...]
        m_new = jnp.maximum(m_prev, jnp.max(s, axis=-1, keepdims=True))
        alpha = jnp.exp(m_prev - m_new)
        p = jnp.exp(s - m_new)
        l_ref[...] = alpha * l_ref[...] + jnp.sum(p, axis=-1, keepdims=True)
        acc_ref[...] = alpha * acc_ref[...] + jnp.dot(p.astype(v_ref.dtype), v_ref[...],
                                                      preferred_element_type=jnp.float32)
        m_ref[...] = m_new

    @pl.when(ik < iq)
    def _off_diag():
        step(False)

    @pl.when(ik == iq)
    def _diag():
        step(True)
        o = acc_ref[...] / l_ref[...]
        lam = _lambda_full(lq_ref, lk_ref, lam_init)
        o = o[0:tq] - lam * o[tq:2 * tq]
        o_ref[...] = _subln(o, g_ref[...], lam_init).astype(o_ref.dtype)


def _flash_prompt(q_bf, k_bf, v_bf, lam_q, lam_k, subln_g, batch, seq, lam_init):
    rows, att_dim = q_bf.shape
    nh = att_dim // V_DIM
    tq = _tile(seq, 512)
    nq = seq // tq
    qspec = pl.BlockSpec((tq, V_DIM), lambda b, h, i, k: (b * nq + i, h))
    kvspec = pl.BlockSpec((tq, V_DIM), lambda b, h, i, k: (b * nq + jnp.minimum(k, i), h))
    small = lambda shape: pl.BlockSpec(shape, lambda b, h, i, k: (0, 0))
    return pl.pallas_call(
        functools.partial(_flash_kernel, tq=tq, lam_init=lam_init),
        grid=(batch, nh, nq, nq),
        in_specs=[qspec, kvspec, kvspec, small((2, QK_DIM)), small((2, QK_DIM)), small((1, V_DIM))],
        out_specs=qspec,
        out_shape=jax.ShapeDtypeStruct((rows, att_dim), jnp.bfloat16),
        scratch_shapes=[
            pltpu.VMEM((2 * tq, V_DIM), jnp.bfloat16),
            pltpu.VMEM((2 * tq, 1), jnp.float32),
            pltpu.VMEM((2 * tq, 1), jnp.float32),
            pltpu.VMEM((2 * tq, V_DIM), jnp.float32),
        ],
        compiler_params=_cparams(("parallel", "parallel", "parallel", "arbitrary")),
        name="flash_prompt",
    )(q_bf, k_bf, v_bf, lam_q, lam_k, subln_g)


def _sample_attn_kernel(pt_ref, qall_ref, knew_ref, vnew_ref, bias_ref, bias_new_ref, lq_ref, lk_ref, g_ref,
                        ck_hbm, cv_hbm, o_ref, kbuf, vbuf, ksem, vsem, m_ref, l_ref, acc_ref,
                        *, n_chunks, pages, page_rows, pool_base, lam_init):
    b = pl.program_id(0)
    c = pl.program_id(1)
    step = b * n_chunks + c
    n_steps = pl.num_programs(0) * n_chunks
    slot = step % 2

    def copies(bb, cc, sl):
        out = []
        for j in range(pages):
            page = pt_ref[bb, cc * pages + j] + pool_base
            dst = pl.ds(j * page_rows, page_rows)
            out.append(pltpu.make_async_copy(ck_hbm.at[page], kbuf.at[sl, dst], ksem.at[sl]))
            out.append(pltpu.make_async_copy(cv_hbm.at[page], vbuf.at[sl, dst], vsem.at[sl]))
        return out

    @pl.when(step == 0)
    def _prime():
        for cp in copies(b, c, slot):
            cp.start()

    @pl.when(step + 1 < n_steps)
    def _prefetch():
        nxt = step + 1
        for cp in copies(nxt // n_chunks, nxt % n_chunks, 1 - slot):
            cp.start()

    @pl.when(c == 0)
    def _init():
        m_ref[...] = jnp.full(m_ref.shape, NEG_BIG, jnp.float32)
        l_ref[...] = jnp.zeros(l_ref.shape, jnp.float32)
        acc_ref[...] = jnp.zeros(acc_ref.shape, jnp.float32)

    for cp in copies(b, c, slot):
        cp.wait()

    qall = qall_ref[0]

    def update(kk, vv, bias):
        s = lax.dot_general(qall, kk, (((1,), (1,)), ((), ())), preferred_element_type=jnp.float32) + bias
        m_prev = m_ref[...]
        m_new = jnp.maximum(m_prev, jnp.max(s, axis=-1, keepdims=True))
        alpha = jnp.exp(m_prev - m_new)
        p = jnp.exp(s - m_new)
        l_ref[...] = alpha * l_ref[...] + jnp.sum(p, axis=-1, keepdims=True)
        acc_ref[...] = alpha * acc_ref[...] + jnp.dot(p.astype(vv.dtype), vv, preferred_element_type=jnp.float32)
        m_ref[...] = m_new

    update(kbuf[slot].astype(jnp.bfloat16), vbuf[slot].astype(jnp.bfloat16), bias_ref[...])

    @pl.when(c == n_chunks - 1)
    def _finish():
        update(knew_ref[0], vnew_ref[0], bias_new_ref[...])
        o = acc_ref[...] / l_ref[...]
        lam = _lambda_full(lq_ref, lk_ref, lam_init)
        g = g_ref[...]
        for h in range(o.shape[0] // 8):
            oh = o[h * 8:(h + 1) * 8]
            oh = oh - lam * pltpu.roll(oh, 4, 0)
            o_ref[0, h * 8:(h + 1) * 8, :] = _subln(oh, g, lam_init)


def _sample_attention(qall, knew, vnew, cache_k2, cache_v2, page_table, layer, n_pool, lam_q, lam_k, subln_g,
                      dec_seq, lam_init):
    nb, nrow, _ = qall.shape
    nh = nrow // 8
    n_pages = page_table.shape[1]
    pages = _tile(n_pages, SAMPLE_PAGES)
    n_chunks = n_pages // pages
    page_rows = PAGE_SIZE * nh
    ncols = pages * page_rows
    row = jnp.arange(nrow)
    col = jnp.arange(ncols)
    bias = jnp.where((col[None, :] % nh) == (row[:, None] // 8), 0.0, NEG_BIG).astype(jnp.float32)
    coln = jnp.arange(dec_seq * nh)
    ok = ((coln[None, :] % nh) == (row[:, None] // 8)) & ((coln[None, :] // nh) <= (row[:, None] % 4))
    bias_new = jnp.where(ok, 0.0, NEG_BIG).astype(jnp.float32)
    const = lambda shape: pl.BlockSpec(shape, lambda b, c, pt: (0,) * len(shape))
    per_seq = lambda shape: pl.BlockSpec(shape, lambda b, c, pt: (b,) + (0,) * (len(shape) - 1))
    grid_spec = pltpu.PrefetchScalarGridSpec(
        num_scalar_prefetch=1,
        grid=(nb, n_chunks),
        in_specs=[
            per_seq((1, nrow, V_DIM)),
            per_seq((1, dec_seq * nh, V_DIM)),
            per_seq((1, dec_seq * nh, V_DIM)),
            const((nrow, ncols)),
            const((nrow, dec_seq * nh)),
            const((2, QK_DIM)),
            const((2, QK_DIM)),
            const((1, V_DIM)),
            pl.BlockSpec(memory_space=pl.ANY),
            pl.BlockSpec(memory_space=pl.ANY),
        ],
        out_specs=per_seq((1, nrow, V_DIM)),
        scratch_shapes=[
            pltpu.VMEM((2, ncols, V_DIM), jnp.float32),
            pltpu.VMEM((2, ncols, V_DIM), jnp.float32),
            pltpu.SemaphoreType.DMA((2,)),
            pltpu.SemaphoreType.DMA((2,)),
            pltpu.VMEM((nrow, 1), jnp.float32),
            pltpu.VMEM((nrow, 1), jnp.float32),
            pltpu.VMEM((nrow, V_DIM), jnp.float32),
        ],
    )
    return pl.pallas_call(
        functools.partial(_sample_attn_kernel, n_chunks=n_chunks, pages=pages, page_rows=page_rows,
                          pool_base=layer * n_pool, lam_init=lam_init),
        grid_spec=grid_spec,
        out_shape=jax.ShapeDtypeStruct((nb, nrow, V_DIM), jnp.float32),
        compiler_params=_cparams(("arbitrary", "arbitrary")),
        name="sample_attn",
    )(page_table, qall, knew, vnew, bias, bias_new, lam_q, lam_k, subln_g, cache_k2, cache_v2)


def _layer_norm(y, g, b):
    mu = jnp.mean(y, axis=-1, keepdims=True)
    yc = y - mu
    var = jnp.mean(yc * yc, axis=-1, keepdims=True)
    return yc * lax.rsqrt(var + LN_EPS) * g + b


def _outproj_kernel(att_ref, conv_ref, x_ref, wa_ref, wc_ref, g_ref, b_ref, wr_ref, br_ref, cnt0_ref,
                    h_ref, ids_ref, rank_ref, gate_ref, cnt_ref, cnt_scr, *, alpha):
    i = pl.program_id(0)

    @pl.when(i == 0)
    def _():
        cnt_scr[...] = cnt0_ref[...]

    mixed = jnp.dot(att_ref[...], wa_ref[...], preferred_element_type=jnp.float32)
    mixed = mixed + jnp.dot(conv_ref[...], wc_ref[...], preferred_element_type=jnp.float32)
    h = _layer_norm(alpha * x_ref[...] + mixed, g_ref[...], b_ref[...])
    h_ref[...] = h

    h_hi = h.astype(jnp.bfloat16)
    h_lo = (h - h_hi.astype(jnp.float32)).astype(jnp.bfloat16)
    wr = wr_ref[...]
    w_hi = wr.astype(jnp.bfloat16)
    w_lo = (wr - w_hi.astype(jnp.float32)).astype(jnp.bfloat16)
    logits = (jnp.dot(h_hi, w_hi, preferred_element_type=jnp.float32)
              + jnp.dot(h_hi, w_lo, preferred_element_type=jnp.float32)
              + jnp.dot(h_lo, w_hi, preferred_element_type=jnp.float32)) + br_ref[...]

    tm, ne = logits.shape
    lane = lax.broadcasted_iota(jnp.int32, (tm, ne), 1).astype(jnp.float32)
    work = logits
    vals, ids, hot = [], [], []
    for _ in range(TOP_K):
        m = jnp.max(work, axis=-1, keepdims=True)
        idx = jnp.min(jnp.where(work == m, lane, float(ne)), axis=-1, keepdims=True)
        sel = lane == idx
        vals.append(m)
        ids.append(idx.astype(jnp.int32))
        hot.append(sel)
        work = jnp.where(sel, -jnp.inf, work)
    exps = [jnp.exp(v - vals[0]) for v in vals]
    denom = exps[0] + exps[1] + exps[2] + exps[3]

    onehot = jnp.where(hot[0] | hot[1] | hot[2] | hot[3], 1.0, 0.0)
    r = lax.broadcasted_iota(jnp.int32, (tm, tm), 0)
    c = lax.broadcasted_iota(jnp.int32, (tm, tm), 1)
    tri = jnp.where(c < r, 1.0, 0.0).astype(jnp.bfloat16)
    before = jnp.dot(tri, onehot.astype(jnp.bfloat16), preferred_element_type=jnp.float32) + cnt_scr[...]
    for k in range(TOP_K):
        ids_ref[:, k:k + 1] = ids[k]
        gate_ref[:, k:k + 1] = exps[k] / denom
        rank = jnp.sum(jnp.where(hot[k], before, 0.0), axis=-1, keepdims=True)
        rank_ref[:, k:k + 1] = rank.astype(jnp.int32)
    cnt_scr[...] = cnt_scr[...] + jnp.sum(onehot, axis=0, keepdims=True)
    cnt_ref[...] = cnt_scr[...]


def _out_proj_route(att_bf, conv_bf, x, w_out_bf, ln_g, ln_b, w_router, b_router, cnt0, alpha):
    rows, d = x.shape
    ka = att_bf.shape[1]
    kc = conv_bf.shape[1]
    ne = w_router.shape[1]
    tm = _tile(rows, 256)
    assert ka == kc
    row = lambda n: pl.BlockSpec((tm, n), lambda i: (i, 0))
    const = lambda shape, idx=(0, 0): pl.BlockSpec(shape, lambda i: idx)
    i32 = jax.ShapeDtypeStruct((rows, TOP_K), jnp.int32)
    return pl.pallas_call(
        functools.partial(_outproj_kernel, alpha=alpha),
        grid=(rows // tm,),
        in_specs=[row(ka), row(kc), row(d), const((ka, d), (0, 0)), const((kc, d), (1, 0)),
                  const((1, d)), const((1, d)), const((d, ne)), const((1, ne)), const((1, ne))],
        out_specs=[row(d), row(TOP_K), row(TOP_K), row(TOP_K), const((1, ne))],
        out_shape=[jax.ShapeDtypeStruct((rows, d), jnp.float32), i32, i32,
                   jax.ShapeDtypeStruct((rows, TOP_K), jnp.float32),
                   jax.ShapeDtypeStruct((1, ne), jnp.float32)],
        scratch_shapes=[pltpu.VMEM((1, ne), jnp.float32)],
        compiler_params=_cparams(("arbitrary",)),
        name="out_proj_route",
    )(att_bf, conv_bf, x, w_out_bf, w_out_bf, ln_g, ln_b, w_router, b_router, cnt0)


def _dispatch_kernel(dest_ref, h_ref, *rest, td):
    xr_ref, sem = rest[-2], rest[-1]
    i = pl.program_id(0)

    def row_copy(t, d):
        return pltpu.make_async_copy(h_ref.at[pl.ds(t, 1)], xr_ref.at[pl.ds(d, 1)], sem.at[0])

    def start(t, carry):
        for k in range(TOP_K):
            row_copy(t, dest_ref[(i * td + t) * TOP_K + k]).start()
        return carry

    def wait(t, carry):
        for k in range(TOP_K):
            row_copy(0, 0).wait()
        return carry

    lax.fori_loop(0, td, start, 0)
    lax.fori_loop(0, td, wait, 0)


def _dispatch(dest_flat, h, xr, n_rows):
    rows, d = h.shape
    td = _tile(rows, DISPATCH_TD)
    in_specs = [pl.BlockSpec((td, d), lambda i, dest: (i, 0))]
    args = [dest_flat, h]
    aliases = {}
    if xr is not None:
        in_specs.append(pl.BlockSpec(memory_space=pl.ANY))
        args.append(xr)
        aliases = {2: 0}
    grid_spec = pltpu.PrefetchScalarGridSpec(
        num_scalar_prefetch=1,
        grid=(rows // td,),
        in_specs=in_specs,
        out_specs=pl.BlockSpec(memory_space=pl.ANY),
        scratch_shapes=[pltpu.SemaphoreType.DMA((1,))],
    )
    return pl.pallas_call(
        functools.partial(_dispatch_kernel, td=td),
        grid_spec=grid_spec,
        out_shape=jax.ShapeDtypeStruct((n_rows, d), jnp.float32),
        input_output_aliases=aliases,
        compiler_params=_cparams(("arbitrary",)),
        name="dispatch",
    )(*args)


def _moe_kernel(ge_ref, gsb_ref, gn_ref, xr_hbm, wg_ref, bg_ref, wu_ref, bu_ref, wd_ref, bd_ref, y_hbm,
                xst, xbf, acc, wgb, wub, wdb, sem, *, sb, nsub, nf):
    g = pl.program_id(0)
    f = pl.program_id(1)
    n = gn_ref[g]
    sb0 = gsb_ref[g]

    def x_copy(s):
        return pltpu.make_async_copy(xr_hbm.at[pl.ds((sb0 + s) * sb, sb)], xst.at[s % 2], sem.at[s % 2])

    def y_copy(s):
        return pltpu.make_async_copy(acc.at[pl.ds(s * sb, sb)], y_hbm.at[pl.ds((sb0 + s) * sb, sb)], sem.at[s % 2])

    @pl.when(n > 0)
    def _group():
        @pl.when(f == 0)
        def _load():
            x_copy(0).start()
            for s in range(nsub):
                @pl.when(s < n)
                def _():
                    if s + 1 < nsub:
                        @pl.when(s + 1 < n)
                        def _():
                            x_copy(s + 1).start()
                    x_copy(s).wait()
                    xbf[s * sb:(s + 1) * sb, :] = xst[s % 2].astype(jnp.bfloat16)

        wgb[...] = wg_ref[0].astype(jnp.bfloat16)
        wub[...] = wu_ref[0].astype(jnp.bfloat16)
        wdb[...] = wd_ref[0].astype(jnp.bfloat16)
        bg = bg_ref[0]
        bu = bu_ref[0]

        for s in range(nsub):
            @pl.when(s < n)
            def _():
                x = xbf[s * sb:(s + 1) * sb, :]
                gg = jnp.dot(x, wgb[...], preferred_element_type=jnp.float32) + bg
                uu = jnp.dot(x, wub[...], preferred_element_type=jnp.float32) + bu
                gg = jnp.minimum(gg, SWIGLU_LIMIT)
                uu = jnp.clip(uu, -SWIGLU_LIMIT, SWIGLU_LIMIT)
                hdn = gg * jax.nn.sigmoid(SWIGLU_ALPHA * gg) * (uu + 1.0)
                contrib = jnp.dot(hdn.astype(jnp.bfloat16), wdb[...], preferred_element_type=jnp.float32)

                @pl.when(f == 0)
                def _():
                    acc[s * sb:(s + 1) * sb, :] = contrib + bd_ref[0]

                @pl.when(f > 0)
                def _():
                    acc[s * sb:(s + 1) * sb, :] += contrib

        @pl.when(f == nf - 1)
        def _store():
            for s in range(nsub):
                @pl.when(s < n)
                def _():
                    y_copy(s).start()
            for s in range(nsub):
                @pl.when(s < n)
                def _():
                    y_copy(s).wait()


def _moe_experts(g_expert, g_sb0, g_nsb, xr, w_gate, b_gate, w_up, b_up, w_down, b_down):
    n_rows, d = xr.shape
    ne, _, dff = w_gate.shape
    fch = _tile(dff, MOE_F)
    nf = dff // fch
    n_groups = g_expert.shape[0]

    def fidx(g, f, ge, gsb, gn):
        return jnp.where(gn[g] > 0, f, nf - 1)

    grid_spec = pltpu.PrefetchScalarGridSpec(
        num_scalar_prefetch=3,
        grid=(n_groups, nf),
        in_specs=[
            pl.BlockSpec(memory_space=pl.ANY),
            pl.BlockSpec((1, d, fch), lambda g, f, ge, gsb, gn: (ge[g], 0, fidx(g, f, ge, gsb, gn))),
            pl.BlockSpec((1, 1, fch), lambda g, f, ge, gsb, gn: (ge[g], 0, fidx(g, f, ge, gsb, gn))),
            pl.BlockSpec((1, d, fch), lambda g, f, ge, gsb, gn: (ge[g], 0, fidx(g, f, ge, gsb, gn))),
            pl.BlockSpec((1, 1, fch), lambda g, f, ge, gsb, gn: (ge[g], 0, fidx(g, f, ge, gsb, gn))),
            pl.BlockSpec((1, fch, d), lambda g, f, ge, gsb, gn: (ge[g], fidx(g, f, ge, gsb, gn), 0)),
            pl.BlockSpec((1, 1, d), lambda g, f, ge, gsb, gn: (ge[g], 0, 0)),
        ],
        out_specs=pl.BlockSpec(memory_space=pl.ANY),
        scratch_shapes=[
            pltpu.VMEM((2, MOE_SB, d), jnp.float32),
            pltpu.VMEM((MOE_NSUB * MOE_SB, d), jnp.bfloat16),
            pltpu.VMEM((MOE_NSUB * MOE_SB, d), jnp.float32),
            pltpu.VMEM((d, fch), jnp.bfloat16),
            pltpu.VMEM((d, fch), jnp.bfloat16),
            pltpu.VMEM((fch, d), jnp.bfloat16),
            pltpu.SemaphoreType.DMA((2,)),
        ],
    )
    return pl.pallas_call(
        functools.partial(_moe_kernel, sb=MOE_SB, nsub=MOE_NSUB, nf=nf),
        grid_spec=grid_spec,
        out_shape=jax.ShapeDtypeStruct((n_rows, d), jnp.float32),
        compiler_params=_cparams(("arbitrary", "arbitrary")),
        name="moe_experts",
    )(g_expert, g_sb0, g_nsb, xr, w_gate, b_gate.reshape(ne, 1, dff), w_up, b_up.reshape(ne, 1, dff),
      w_down, b_down.reshape(ne, 1, d))


def _combine_kernel(dest_ref, h_ref, gate_ref, g_ref, b_ref, y_hbm, o_ref, gbuf, sem, *, td, alpha):
    i = pl.program_id(0)
    n = pl.num_programs(0)
    slot = i % 2

    def row_copy(step, t, k, sl):
        d = dest_ref[(step * td + t) * TOP_K + k]
        return pltpu.make_async_copy(y_hbm.at[pl.ds(d, 1)], gbuf.at[sl, k, pl.ds(t, 1)], sem.at[sl])

    def issue(step, sl):
        def body(t, carry):
            for k in range(TOP_K):
                row_copy(step, t, k, sl).start()
            return carry
        lax.fori_loop(0, td, body, 0)

    @pl.when(i == 0)
    def _():
        issue(0, 0)

    @pl.when(i + 1 < n)
    def _():
        issue(i + 1, 1 - slot)

    def wait(t, carry):
        for k in range(TOP_K):
            pltpu.make_async_copy(y_hbm.at[pl.ds(0, 1)], gbuf.at[slot, k, pl.ds(0, 1)], sem.at[slot]).wait()
        return carry
    lax.fori_loop(0, td, wait, 0)

    gates = gate_ref[...]
    moe = gates[:, 0:1] * gbuf[slot, 0]
    for k in range(1, TOP_K):
        moe = moe + gates[:, k:k + 1] * gbuf[slot, k]
    o_ref[...] = _layer_norm(alpha * h_ref[...] + moe, g_ref[...], b_ref[...])


def _combine(dest_flat, h, gates, ln_g, ln_b, yr, alpha):
    rows, d = h.shape
    td = _tile(rows, DISPATCH_TD)
    grid_spec = pltpu.PrefetchScalarGridSpec(
        num_scalar_prefetch=1,
        grid=(rows // td,),
        in_specs=[
            pl.BlockSpec((td, d), lambda i, dest: (i, 0)),
            pl.BlockSpec((td, TOP_K), lambda i, dest: (i, 0)),
            pl.BlockSpec((1, d), lambda i, dest: (0, 0)),
            pl.BlockSpec((1, d), lambda i, dest: (0, 0)),
            pl.BlockSpec(memory_space=pl.ANY),
        ],
        out_specs=pl.BlockSpec((td, d), lambda i, dest: (i, 0)),
        scratch_shapes=[pltpu.VMEM((2, TOP_K, td, d), jnp.float32), pltpu.SemaphoreType.DMA((2,))],
    )
    return pl.pallas_call(
        functools.partial(_combine_kernel, td=td, alpha=alpha),
        grid_spec=grid_spec,
        out_shape=jax.ShapeDtypeStruct((rows, d), jnp.float32),
        compiler_params=_cparams(("arbitrary",)),
        name="combine",
    )(dest_flat, h, gates, ln_g, ln_b, yr)


def _rope_tables(pos):
    half = QK_DIM // 2
    inv = jnp.exp(-math.log(ROPE_THETA) * jnp.arange(half, dtype=jnp.float32) * (2.0 / QK_DIM))
    ang = pos.astype(jnp.float32)[:, None] * inv[None, :]
    cos, sin = jnp.cos(ang), jnp.sin(ang)
    reps = LANES // QK_DIM
    return jnp.tile(jnp.concatenate([cos, cos], axis=-1), (1, reps)), \
        jnp.tile(jnp.concatenate([-sin, sin], axis=-1), (1, reps))


def _shifted_taps(u, state):
    b, s, c = u.shape
    full = jnp.concatenate([state, u], axis=1)
    taps = [full[:, j:j + s].reshape(b * s, c) for j in range(CONV_K)]
    return taps, full[:, s:]


def _routing_tables(counts, n_groups):
    nsb = (counts + MOE_SB - 1) // MOE_SB
    sb_end = jnp.cumsum(nsb)
    sb_off = sb_end - nsb
    ng = (nsb + MOE_NSUB - 1) // MOE_NSUB
    g_end = jnp.cumsum(ng)
    g_off = g_end - ng
    gi = jnp.arange(n_groups, dtype=jnp.int32)
    last = jnp.maximum(g_end[-1] - 1, 0)
    ge = jnp.searchsorted(g_end, jnp.minimum(gi, last), side='right').astype(jnp.int32)
    ge = jnp.minimum(ge, counts.shape[0] - 1)
    local = gi - g_off[ge]
    g_sb0 = sb_off[ge] + local * MOE_NSUB
    g_nsb = jnp.where(gi < g_end[-1], jnp.clip(nsb[ge] - local * MOE_NSUB, 0, MOE_NSUB), 0)
    return sb_off * MOE_SB, ge, g_sb0.astype(jnp.int32), g_nsb.astype(jnp.int32)


def kernel(x_prompt, x_sample, cache_k, cache_v, state_conv, page_table, w_in, conv_w, lambda_q, lambda_k, subln_g, w_out, ln1_g, ln1_b, w_router, b_router, w_gate, b_gate, w_up, b_up, w_down, b_down, ln2_g, ln2_b):
    bp, sp, d = x_prompt.shape
    bs, ts, _ = x_sample.shape
    depth, n_pool = cache_k.shape[0], cache_k.shape[1]
    nh = cache_k.shape[3]
    att_dim = nh * V_DIM
    conv_dim = conv_w.shape[-1]
    ne = w_router.shape[-1]
    n_past = page_table.shape[1] * PAGE_SIZE
    alpha = (2.0 * depth) ** 0.25
    assert ts == 4 and cache_k.shape[2] == PAGE_SIZE and w_in.shape[-1] == 3 * att_dim + 3 * conv_dim

    cos_p, sin_p = _rope_tables(jnp.tile(jnp.arange(sp), bp))
    cos_s, sin_s = _rope_tables(jnp.tile(n_past + jnp.arange(ts), bs))
    cache_k2 = cache_k.reshape(depth * n_pool, PAGE_SIZE * nh, V_DIM)
    cache_v2 = cache_v.reshape(depth * n_pool, PAGE_SIZE * nh, V_DIM)

    n_tok = bp * sp + bs * ts
    n_rows = (n_tok * TOP_K // MOE_SB + ne) * MOE_SB
    n_groups = n_rows // (MOE_SB * MOE_NSUB) + ne + 1

    xp = x_prompt.reshape(bp * sp, d)
    xs = x_sample.reshape(bs * ts, d)
    outs = {k: [] for k in ('kp', 'vp', 'cp', 'ks', 'vs', 'cs')}
    for l in range(depth):
        lam_init = 0.8 - 0.6 * math.exp(-0.3 * l)
        w_in_bf = w_in[l].astype(jnp.bfloat16)
        w_out_bf = w_out[l].astype(jnp.bfloat16)
        lq, lk, sg = lambda_q[l], lambda_k[l], subln_g[l].reshape(1, V_DIM)
        g1, b1 = ln1_g[l].reshape(1, d), ln1_b[l].reshape(1, d)
        g2, b2 = ln2_g[l].reshape(1, d), ln2_b[l].reshape(1, d)
        br = b_router[l].reshape(1, ne)

        xp_bf = xp.astype(jnp.bfloat16)
        q_bf, k_f, k_bf, v_f, v_bf = _qkv_proj(xp_bf, w_in_bf, cos_p, sin_p, att_dim)
        u_p, gb_p = _conv_proj(xp_bf, w_in_bf, att_dim, conv_dim)
        att_p = _flash_prompt(q_bf, k_bf, v_bf, lq, lk, sg, bp, sp, lam_init)
        taps, cst_p = _shifted_taps(u_p.reshape(bp, sp, conv_dim), jnp.zeros((bp, CONV_K - 1, conv_dim), u_p.dtype))
        conv_p = _conv_gate(*taps, gb_p, conv_w[l])
        outs['kp'].append(k_f.reshape(bp, sp, nh, V_DIM))
        outs['vp'].append(v_f.reshape(bp, sp, nh, V_DIM))
        outs['cp'].append(cst_p)

        xs_bf = xs.astype(jnp.bfloat16)
        qs_bf, ks_f, ks_bf, vs_f, vs_bf = _qkv_proj(xs_bf, w_in_bf, cos_s, sin_s, att_dim)
        u_s, gb_s = _conv_proj(xs_bf, w_in_bf, att_dim, conv_dim)
        q5 = qs_bf.reshape(bs, ts, nh, 2, QK_DIM)
        eye = jnp.eye(2, dtype=q5.dtype)
        qall = jnp.einsum('bthmd,mn->bhmtnd', q5, eye).reshape(bs, nh * 2 * ts, V_DIM)
        att_s = _sample_attention(qall, ks_bf.reshape(bs, ts * nh, V_DIM), vs_bf.reshape(bs, ts * nh, V_DIM),
                                  cache_k2, cache_v2, page_table, l, n_pool, lq, lk, sg, ts, lam_init)
        att_s = att_s.reshape(bs, nh, 2 * ts, V_DIM)[:, :, :ts]
        att_s = jnp.transpose(att_s, (0, 2, 1, 3)).reshape(bs * ts, att_dim).astype(jnp.bfloat16)
        taps, cst_s = _shifted_taps(u_s.reshape(bs, ts, conv_dim), state_conv[l])
        conv_s = _conv_gate(*taps, gb_s, conv_w[l])
        outs['ks'].append(ks_f.reshape(bs, ts, nh, V_DIM))
        outs['vs'].append(vs_f.reshape(bs, ts, nh, V_DIM))
        outs['cs'].append(cst_s)

        cnt0 = jnp.zeros((1, ne), jnp.float32)
        h_p, ids_p, rank_p, gate_p, cnt_p = _out_proj_route(att_p, conv_p, xp, w_out_bf, g1, b1, w_router[l], br, cnt0, alpha)
        h_s, ids_s, rank_s, gate_s, cnt_all = _out_proj_route(att_s, conv_s, xs, w_out_bf, g1, b1, w_router[l], br, cnt_p, alpha)

        counts = cnt_all.reshape(ne).astype(jnp.int32)
        row_off, g_expert, g_sb0, g_nsb = _routing_tables(counts, n_groups)
        dest_p = (row_off[ids_p] + rank_p).reshape(-1).astype(jnp.int32)
        dest_s = (row_off[ids_s] + rank_s).reshape(-1).astype(jnp.int32)

        xr = _dispatch(dest_p, h_p, None, n_rows)
        xr = _dispatch(dest_s, h_s, xr, n_rows)
        yr = _moe_experts(g_expert, g_sb0, g_nsb, xr, w_gate[l], b_gate[l], w_up[l], b_up[l], w_down[l], b_down[l])
        xp = _combine(dest_p, h_p, gate_p, g2, b2, yr, alpha)
        xs = _combine(dest_s, h_s, gate_s, g2, b2, yr, alpha)

    return (xp.reshape(bp, sp, d), xs.reshape(bs, ts, d),
            jnp.stack(outs['kp']), jnp.stack(outs['vp']), jnp.stack(outs['cp']),
            jnp.stack(outs['ks']), jnp.stack(outs['vs']), jnp.stack(outs['cs']))
```

```python
import functools
import math

import jax
import jax.numpy as jnp
from jax import lax
from jax.experimental import pallas as pl
from jax.experimental.pallas import tpu as pltpu

QK_DIM = 64
V_DIM = 2 * QK_DIM
CONV_K = 3
PAGE_SIZE = 128
ROPE_THETA = 10000.0
SUBLN_EPS = 1e-5
LN_EPS = 1e-5
TOP_K = 4
SWIGLU_ALPHA = 1.702
SWIGLU_LIMIT = 7.0

LANES = 128
NEG_BIG = -1e30
Q_SCALE = QK_DIM ** -0.5 * math.log2(math.e)

FLASH_T = 512
FLASH_HEADS = 4
FLASH_CW = 256

VMEM_LIMIT = 56 * 1024 * 1024

MOE_SB = 256
MOE_NSUB = 8
MOE_F = 256
MOE_FN = 512
DISPATCH_TD = 128
SAMPLE_PAGES = 16
SAMPLE_SUB_PAGES = 8


def _cparams(sem):
    return pltpu.CompilerParams(dimension_semantics=sem, vmem_limit_bytes=VMEM_LIMIT)


def _tile(n, pref):
    t = min(n, pref)
    while n % t:
        t //= 2
    return t


def _rope_chunks(a, cos, sin_signed):
    lane = lax.broadcasted_iota(jnp.int32, (a.shape[0], LANES), 1)
    first = (lane % QK_DIM) < (QK_DIM // 2)
    outs = []
    for c in range(a.shape[1] // LANES):
        ac = a[:, c * LANES:(c + 1) * LANES]
        partner = jnp.where(first, pltpu.roll(ac, LANES - QK_DIM // 2, 1), pltpu.roll(ac, QK_DIM // 2, 1))
        outs.append(ac * cos + partner * sin_signed)
    return outs


def _qkv_kernel(x_ref, w_ref, cos_ref, sin_ref, qb_ref, kf_ref, kb_ref, vf_ref, vb_ref, vt_ref, *, nsec, rc):
    j = pl.program_id(1)
    tm = x_ref.shape[0]

    def chunks(epilogue):
        for r in range(tm // rc):
            rs = slice(r * rc, (r + 1) * rc)
            epilogue(rs, jnp.dot(x_ref[rs, :], w_ref[...], preferred_element_type=jnp.float32))

    @pl.when(j < nsec)
    def _q():
        def epilogue(rs, acc):
            for c, o in enumerate(_rope_chunks(acc, cos_ref[rs, :], sin_ref[rs, :])):
                qb_ref[rs, c * LANES:(c + 1) * LANES] = (o * Q_SCALE).astype(qb_ref.dtype)
        chunks(epilogue)

    @pl.when((j >= nsec) & (j < 2 * nsec))
    def _k():
        def epilogue(rs, acc):
            for c, o in enumerate(_rope_chunks(acc, cos_ref[rs, :], sin_ref[rs, :])):
                kf_ref[rs, c * LANES:(c + 1) * LANES] = o
                kb_ref[rs, c * LANES:(c + 1) * LANES] = o.astype(kb_ref.dtype)
        chunks(epilogue)

    @pl.when(j >= 2 * nsec)
    def _v():
        def epilogue(rs, acc):
            vf_ref[rs, :] = acc
            vb_ref[rs, :] = acc.astype(vb_ref.dtype)
            vt_ref[:, rs] = acc.T.astype(vt_ref.dtype)
        chunks(epilogue)


def _qkv_proj(x_bf, w_bf, cos, sin_signed, att_dim):
    rows, d = x_bf.shape
    tm = _tile(rows, 1024)
    tn = _tile(att_dim, 512)
    nsec = att_dim // tn
    table_blocks = cos.shape[0] // tm
    assert cos.shape[0] % tm == 0
    sec = lambda s: (lambda i, j: (i, jnp.clip(j - s * nsec, 0, nsec - 1)))
    f32 = jax.ShapeDtypeStruct((rows, att_dim), jnp.float32)
    b16 = jax.ShapeDtypeStruct((rows, att_dim), jnp.bfloat16)
    return pl.pallas_call(
        functools.partial(_qkv_kernel, nsec=nsec, rc=_tile(tm, 256)),
        grid=(rows // tm, 3 * nsec),
        in_specs=[
            pl.BlockSpec((tm, d), lambda i, j: (i, 0)),
            pl.BlockSpec((d, tn), lambda i, j: (0, j)),
            pl.BlockSpec((tm, LANES), lambda i, j: (i % table_blocks, 0)),
            pl.BlockSpec((tm, LANES), lambda i, j: (i % table_blocks, 0)),
        ],
        out_specs=[
            pl.BlockSpec((tm, tn), sec(0)),
            pl.BlockSpec((tm, tn), sec(1)),
            pl.BlockSpec((tm, tn), sec(1)),
            pl.BlockSpec((tm, tn), sec(2)),
            pl.BlockSpec((tm, tn), sec(2)),
            pl.BlockSpec((tn, tm), lambda i, j: (jnp.clip(j - 2 * nsec, 0, nsec - 1), i)),
        ],
        out_shape=[b16, f32, b16, f32, b16, jax.ShapeDtypeStruct((att_dim, rows), jnp.bfloat16)],
        compiler_params=_cparams(("parallel", "arbitrary")),
        name="qkv_proj",
    )(x_bf, w_bf, cos, sin_signed)


def _convproj_kernel(x_ref, wh_ref, wb_ref, wc_ref, u_ref, gb_ref):
    x = x_ref[...]
    h = jnp.dot(x, wh_ref[...], preferred_element_type=jnp.float32)
    gc = jnp.dot(x, wc_ref[...], preferred_element_type=jnp.float32)
    u_ref[...] = gc * h
    gb_ref[...] = jnp.dot(x, wb_ref[...], preferred_element_type=jnp.float32)


def _conv_proj(x_bf, w_bf, att_dim, conv_dim):
    rows, d = x_bf.shape
    tm = _tile(rows, 1024)
    tn = _tile(conv_dim, 512)
    base = 3 * att_dim // tn
    nsec = conv_dim // tn
    wspec = lambda s: pl.BlockSpec((d, tn), lambda i, j: (0, base + s * nsec + j))
    f32 = jax.ShapeDtypeStruct((rows, conv_dim), jnp.float32)
    return pl.pallas_call(
        _convproj_kernel,
        grid=(rows // tm, nsec),
        in_specs=[pl.BlockSpec((tm, d), lambda i, j: (i, 0)), wspec(0), wspec(1), wspec(2)],
        out_specs=[pl.BlockSpec((tm, tn), lambda i, j: (i, j))] * 2,
        out_shape=[f32, f32],
        compiler_params=_cparams(("parallel", "arbitrary")),
        name="conv_proj",
    )(x_bf, w_bf, w_bf, w_bf)


def _conv_gate_kernel(s0_ref, s1_ref, s2_ref, gb_ref, w_ref, o_ref):
    w = w_ref[...]
    y = s0_ref[...] * w[0:1] + s1_ref[...] * w[1:2] + s2_ref[...] * w[2:3]
    o_ref[...] = (gb_ref[...] * y).astype(o_ref.dtype)


def _conv_gate(s0, s1, s2, gb, conv_w):
    rows, c = gb.shape
    tm = _tile(rows, 512)
    blk = pl.BlockSpec((tm, c), lambda i: (i, 0))
    return pl.pallas_call(
        _conv_gate_kernel,
        grid=(rows // tm,),
        in_specs=[blk, blk, blk, blk, pl.BlockSpec((CONV_K, c), lambda i: (0, 0))],
        out_specs=blk,
        out_shape=jax.ShapeDtypeStruct((rows, c), jnp.bfloat16),
        compiler_params=_cparams(("parallel",)),
        name="conv_gate",
    )(s0, s1, s2, gb, conv_w)


def _lambda_full(lq_ref, lk_ref, lam_init):
    prod = lq_ref[...] * lk_ref[...]
    e = jnp.exp(jnp.sum(prod, axis=-1, keepdims=True))
    return e[0:1] - e[1:2] + lam_init


def _subln(o, g, lam_init):
    ms = jnp.mean(o * o, axis=-1, keepdims=True)
    return o * lax.rsqrt(ms + SUBLN_EPS) * g * (1.0 - lam_init)


def _flash_kernel(iq_ref, ik_ref, q_ref, k_ref, vt_ref, lq_ref, lk_ref, g_ref, o_ref, qq_ref, m_ref, l_ref, acc_ref,
                  *, tq, hp, cw, lam_init):
    pair = pl.program_id(2)
    iq = iq_ref[pair]
    ik = ik_ref[pair]
    nc = 2 * tq // cw

    @pl.when(ik == 0)
    def _init():
        for h in range(hp):
            q = q_ref[:, h * V_DIM:(h + 1) * V_DIM]
            lane = lax.broadcasted_iota(jnp.int32, q.shape, 1)
            zero = jnp.zeros_like(q)
            qq_ref[h, 0:tq] = jnp.where(lane < QK_DIM, q, zero)
            qq_ref[h, tq:2 * tq] = jnp.where(lane >= QK_DIM, q, zero)
        m_ref[...] = jnp.full(m_ref.shape, NEG_BIG, jnp.float32)
        l_ref[...] = jnp.zeros(l_ref.shape, jnp.float32)
        acc_ref[...] = jnp.zeros(acc_ref.shape, jnp.float32)

    def step(masked):
        scores = []
        for h in range(hp):
            kk = k_ref[:, h * V_DIM:(h + 1) * V_DIM]
            for c in range(nc):
                s = lax.dot_general(kk, qq_ref[h, c * cw:(c + 1) * cw, :], (((1,), (1,)), ((), ())),
                                    preferred_element_type=jnp.float32)
                if masked:
                    key = lax.broadcasted_iota(jnp.int32, s.shape, 0)
                    qry = lax.broadcasted_iota(jnp.int32, s.shape, 1) + ((c * cw) % tq)
                    s = jnp.where(key <= qry, s, NEG_BIG)
                scores.append(s)
        for h in range(hp):
            vt = vt_ref[h * V_DIM:(h + 1) * V_DIM, :]
            m_all = m_ref[h]
            l_all = l_ref[h]
            m_out, l_out = [], []
            for c in range(nc):
                cs = slice(c * cw, (c + 1) * cw)
                s = scores[h * nc + c]
                m_prev = m_all[:, cs]
                m_new = jnp.maximum(m_prev, jnp.max(s, axis=0, keepdims=True))
                alpha = jnp.exp2(m_prev - m_new)
                p = jnp.exp2(s - m_new)
                l_out.append(alpha * l_all[:, cs] + jnp.sum(p, axis=0, keepdims=True))
                m_out.append(m_new)
                acc_ref[h, :, cs] = alpha * acc_ref[h, :, cs] + jnp.dot(vt, p.astype(vt.dtype),
                                                                        preferred_element_type=jnp.float32)
            m_ref[h] = jnp.concatenate(m_out, axis=1)
            l_ref[h] = jnp.concatenate(l_out, axis=1)

    @pl.when(ik < iq)
    def _off_diag():
        step(False)

    @pl.when(ik == iq)
    def _diag():
        step(True)
        lam = _lambda_full(lq_ref, lk_ref, lam_init)
        for h in range(hp):
            o = acc_ref[h] / l_ref[h]
            o = o[:, 0:tq] - lam * o[:, tq:2 * tq]
            ms = jnp.mean(o * o, axis=0, keepdims=True)
            o = o * lax.rsqrt(ms + SUBLN_EPS) * g_ref[...] * (1.0 - lam_init)
            o_ref[:, h * V_DIM:(h + 1) * V_DIM] = o.T.astype(o_ref.dtype)


def _flash_prompt(q_bf, k_bf, vt_bf, lam_q, lam_k, subln_g_col, batch, seq, lam_init):
    rows, att_dim = q_bf.shape
    nh = att_dim // V_DIM
    hp = _tile(nh, FLASH_HEADS)
    tq = _tile(seq, FLASH_T)
    cw = _tile(2 * tq, FLASH_CW)
    nq = seq // tq
    pairs = [(i, k) for i in range(nq) for k in range(i + 1)]
    iq_tab = jnp.asarray([p[0] for p in pairs], jnp.int32)
    ik_tab = jnp.asarray([p[1] for p in pairs], jnp.int32)
    qspec = pl.BlockSpec((tq, hp * V_DIM), lambda b, h, p, iqt, ikt: (b * nq + iqt[p], h))
    kspec = pl.BlockSpec((tq, hp * V_DIM), lambda b, h, p, iqt, ikt: (b * nq + ikt[p], h))
    vspec = pl.BlockSpec((hp * V_DIM, tq), lambda b, h, p, iqt, ikt: (h, b * nq + ikt[p]))
    small = lambda shape: pl.BlockSpec(shape, lambda b, h, p, iqt, ikt: (0, 0))
    grid_spec = pltpu.PrefetchScalarGridSpec(
        num_scalar_prefetch=2,
        grid=(batch, nh // hp, len(pairs)),
        in_specs=[qspec, kspec, vspec, small((2, QK_DIM)), small((2, QK_DIM)), small((V_DIM, 1))],
        out_specs=qspec,
        scratch_shapes=[
            pltpu.VMEM((hp, 2 * tq, V_DIM), jnp.bfloat16),
            pltpu.VMEM((hp, 1, 2 * tq), jnp.float32),
            pltpu.VMEM((hp, 1, 2 * tq), jnp.float32),
            pltpu.VMEM((hp, V_DIM, 2 * tq), jnp.float32),
        ],
    )
    return pl.pallas_call(
        functools.partial(_flash_kernel, tq=tq, hp=hp, cw=cw, lam_init=lam_init),
        grid_spec=grid_spec,
        out_shape=jax.ShapeDtypeStruct((rows, att_dim), jnp.bfloat16),
        compiler_params=_cparams(("parallel", "parallel", "arbitrary")),
        name="flash_prompt",
    )(iq_tab, ik_tab, q_bf, k_bf, vt_bf, lam_q, lam_k, subln_g_col)


def _sample_attn_kernel(pt_ref, qall_ref, knew_ref, vnew_ref, bias_ref, bias_new_ref, lq_ref, lk_ref, g_ref,
                        ck_hbm, cv_hbm, o_ref, kbuf, vbuf, ksem, vsem, m_ref, l_ref, acc_ref,
                        *, n_chunks, pages, page_rows, pool_base, lam_init):
    b = pl.program_id(0)
    c = pl.program_id(1)
    step = b * n_chunks + c
    n_steps = pl.num_programs(0) * n_chunks
    slot = step % 2

    def copies(bb, cc, sl):
        out = []
        for j in range(pages):
            page = pt_ref[bb, cc * pages + j] + pool_base
            dst = pl.ds(j * page_rows, page_rows)
            out.append(pltpu.make_async_copy(ck_hbm.at[page], kbuf.at[sl, dst], ksem.at[sl]))
            out.append(pltpu.make_async_copy(cv_hbm.at[page], vbuf.at[sl, dst], vsem.at[sl]))
        return out

    @pl.when(step == 0)
    def _prime():
        for cp in copies(b, c, slot):
            cp.start()

    @pl.when(step + 1 < n_steps)
    def _prefetch():
        nxt = step + 1
        for cp in copies(nxt // n_chunks, nxt % n_chunks, 1 - slot):
            cp.start()

    @pl.when(c == 0)
    def _init():
        m_ref[...] = jnp.full(m_ref.shape, NEG_BIG, jnp.float32)
        l_ref[...] = jnp.zeros(l_ref.shape, jnp.float32)
        acc_ref[...] = jnp.zeros(acc_ref.shape, jnp.float32)

    for cp in copies(b, c, slot):
        cp.wait()

    qall = qall_ref[0]

    def update(kk, vv, bias):
        s = lax.dot_general(qall, kk, (((1,), (1,)), ((), ())), preferred_element_type=jnp.float32) + bias
        m_prev = m_ref[...]
        m_new = jnp.maximum(m_prev, jnp.max(s, axis=-1, keepdims=True))
        alpha = jnp.exp2(m_prev - m_new)
        p = jnp.exp2(s - m_new)
        l_ref[...] = alpha * l_ref[...] + jnp.sum(p, axis=-1, keepdims=True)
        acc_ref[...] = alpha * acc_ref[...] + jnp.dot(p.astype(vv.dtype), vv, preferred_element_type=jnp.float32)
        m_ref[...] = m_new

    sub = bias_ref.shape[1]
    for u in range(pages * page_rows // sub):
        rows = pl.ds(u * sub, sub)
        update(kbuf[slot, rows, :].astype(jnp.bfloat16), vbuf[slot, rows, :].astype(jnp.bfloat16), bias_ref[...])

    @pl.when(c == n_chunks - 1)
    def _finish():
        update(knew_ref[0], vnew_ref[0], bias_new_ref[...])
        o = acc_ref[...] / l_ref[...]
        lam = _lambda_full(lq_ref, lk_ref, lam_init)
        g = g_ref[...]
        for h in range(o.shape[0] // 8):
            oh = o[h * 8:(h + 1) * 8]
            oh = oh - lam * pltpu.roll(oh, 4, 0)
            o_ref[0, h * 8:(h + 1) * 8, :] = _subln(oh, g, lam_init)


def _sample_attention(qall, knew, vnew, cache_k2, cache_v2, page_table, layer, n_pool, lam_q, lam_k, subln_g,
                      dec_seq, lam_init):
    nb, nrow, _ = qall.shape
    nh = nrow // 8
    n_pages = page_table.shape[1]
    pages = _tile(n_pages, SAMPLE_PAGES)
    n_chunks = n_pages // pages
    page_rows = PAGE_SIZE * nh
    ncols = pages * page_rows
    sub_cols = _tile(pages, SAMPLE_SUB_PAGES) * page_rows
    row = jnp.arange(nrow)
    col = jnp.arange(sub_cols)
    bias = jnp.where((col[None, :] % nh) == (row[:, None] // 8), 0.0, NEG_BIG).astype(jnp.float32)
    coln = jnp.arange(dec_seq * nh)
    ok = ((coln[None, :] % nh) == (row[:, None] // 8)) & ((coln[None, :] // nh) <= (row[:, None] % 4))
    bias_new = jnp.where(ok, 0.0, NEG_BIG).astype(jnp.float32)
    const = lambda shape: pl.BlockSpec(shape, lambda b, c, pt: (0,) * len(shape))
    per_seq = lambda shape: pl.BlockSpec(shape, lambda b, c, pt: (b,) + (0,) * (len(shape) - 1))
    grid_spec = pltpu.PrefetchScalarGridSpec(
        num_scalar_prefetch=1,
        grid=(nb, n_chunks),
        in_specs=[
            per_seq((1, nrow, V_DIM)),
            per_seq((1, dec_seq * nh, V_DIM)),
            per_seq((1, dec_seq * nh, V_DIM)),
            const((nrow, sub_cols)),
            const((nrow, dec_seq * nh)),
            const((2, QK_DIM)),
            const((2, QK_DIM)),
            const((1, V_DIM)),
            pl.BlockSpec(memory_space=pl.ANY),
            pl.BlockSpec(memory_space=pl.ANY),
        ],
        out_specs=per_seq((1, nrow, V_DIM)),
        scratch_shapes=[
            pltpu.VMEM((2, ncols, V_DIM), jnp.float32),
            pltpu.VMEM((2, ncols, V_DIM), jnp.float32),
            pltpu.SemaphoreType.DMA((2,)),
            pltpu.SemaphoreType.DMA((2,)),
            pltpu.VMEM((nrow, 1), jnp.float32),
            pltpu.VMEM((nrow, 1), jnp.float32),
            pltpu.VMEM((nrow, V_DIM), jnp.float32),
        ],
    )
    return pl.pallas_call(
        functools.partial(_sample_attn_kernel, n_chunks=n_chunks, pages=pages, page_rows=page_rows,
                          pool_base=layer * n_pool, lam_init=lam_init),
        grid_spec=grid_spec,
        out_shape=jax.ShapeDtypeStruct((nb, nrow, V_DIM), jnp.float32),
        compiler_params=_cparams(("arbitrary", "arbitrary")),
        name="sample_attn",
    )(page_table, qall, knew, vnew, bias, bias_new, lam_q, lam_k, subln_g, cache_k2, cache_v2)


def _layer_norm(y, g, b):
    mu = jnp.mean(y, axis=-1, keepdims=True)
    yc = y - mu
    var = jnp.mean(yc * yc, axis=-1, keepdims=True)
    return yc * lax.rsqrt(var + LN_EPS) * g + b


def _outproj_kernel(att_ref, conv_ref, x_ref, wa_ref, wc_ref, g_ref, b_ref, wr_ref, br_ref, cnt0_ref,
                    h_ref, ids_ref, rank_ref, gate_ref, cnt_ref, cnt_scr, *, alpha, rc):
    i = pl.program_id(0)

    @pl.when(i == 0)
    def _():
        cnt_scr[...] = cnt0_ref[...]

    wr = wr_ref[...]
    w_hi = wr.astype(jnp.bfloat16)
    w_lo = (wr - w_hi.astype(jnp.float32)).astype(jnp.bfloat16)
    ne = wr.shape[1]
    lane = lax.broadcasted_iota(jnp.int32, (rc, ne), 1).astype(jnp.float32)
    r = lax.broadcasted_iota(jnp.int32, (rc, rc), 0)
    c = lax.broadcasted_iota(jnp.int32, (rc, rc), 1)
    tri = jnp.where(c < r, 1.0, 0.0).astype(jnp.bfloat16)
    cnt = cnt_scr[...]

    for q in range(att_ref.shape[0] // rc):
        rs = slice(q * rc, (q + 1) * rc)
        mixed = jnp.dot(att_ref[rs, :], wa_ref[...], preferred_element_type=jnp.float32)
        mixed = mixed + jnp.dot(conv_ref[rs, :], wc_ref[...], preferred_element_type=jnp.float32)
        h = _layer_norm(alpha * x_ref[rs, :] + mixed, g_ref[...], b_ref[...])
        h_ref[rs, :] = h

        h_hi = h.astype(jnp.bfloat16)
        h_lo = (h - h_hi.astype(jnp.float32)).astype(jnp.bfloat16)
        logits = (jnp.dot(h_hi, w_hi, preferred_element_type=jnp.float32)
                  + jnp.dot(h_hi, w_lo, preferred_element_type=jnp.float32)
                  + jnp.dot(h_lo, w_hi, preferred_element_type=jnp.float32)) + br_ref[...]

        work = logits
        vals, ids, hot = [], [], []
        for _ in range(TOP_K):
            m = jnp.max(work, axis=-1, keepdims=True)
            idx = jnp.min(jnp.where(work == m, lane, float(ne)), axis=-1, keepdims=True)
            sel = lane == idx
            vals.append(m)
            ids.append(idx.astype(jnp.int32))
            hot.append(sel)
            work = jnp.where(sel, -jnp.inf, work)
        exps = [jnp.exp(v - vals[0]) for v in vals]
        denom = exps[0] + exps[1] + exps[2] + exps[3]

        onehot = jnp.where(hot[0] | hot[1] | hot[2] | hot[3], 1.0, 0.0)
        before = jnp.dot(tri, onehot.astype(jnp.bfloat16), preferred_element_type=jnp.float32) + cnt
        for k in range(TOP_K):
            ids_ref[rs, k:k + 1] = ids[k]
            gate_ref[rs, k:k + 1] = exps[k] / denom
            rank = jnp.sum(jnp.where(hot[k], before, 0.0), axis=-1, keepdims=True)
            rank_ref[rs, k:k + 1] = rank.astype(jnp.int32)
        cnt = cnt + jnp.sum(onehot, axis=0, keepdims=True)

    cnt_scr[...] = cnt
    cnt_ref[...] = cnt


def _out_proj_route(att_bf, conv_bf, x, w_out_bf, ln_g, ln_b, w_router, b_router, cnt0, alpha):
    rows, d = x.shape
    ka = att_bf.shape[1]
    kc = conv_bf.shape[1]
    ne = w_router.shape[1]
    tm = _tile(rows, 512)
    rc = _tile(tm, 256)
    assert ka == kc
    row = lambda n: pl.BlockSpec((tm, n), lambda i: (i, 0))
    const = lambda shape, idx=(0, 0): pl.BlockSpec(shape, lambda i: idx)
    i32 = jax.ShapeDtypeStruct((rows, TOP_K), jnp.int32)
    return pl.pallas_call(
        functools.partial(_outproj_kernel, alpha=alpha, rc=rc),
        grid=(rows // tm,),
        in_specs=[row(ka), row(kc), row(d), const((ka, d), (0, 0)), const((kc, d), (1, 0)),
                  const((1, d)), const((1, d)), const((d, ne)), const((1, ne)), const((1, ne))],
        out_specs=[row(d), row(TOP_K), row(TOP_K), row(TOP_K), const((1, ne))],
        out_shape=[jax.ShapeDtypeStruct((rows, d), jnp.float32), i32, i32,
                   jax.ShapeDtypeStruct((rows, TOP_K), jnp.float32),
                   jax.ShapeDtypeStruct((1, ne), jnp.float32)],
        scratch_shapes=[pltpu.VMEM((1, ne), jnp.float32)],
        compiler_params=_cparams(("arbitrary",)),
        name="out_proj_route",
    )(att_bf, conv_bf, x, w_out_bf, w_out_bf, ln_g, ln_b, w_router, b_router, cnt0)


def _dispatch_kernel(dest_ref, h_ref, *rest, td):
    xr_ref, sem = rest[-2], rest[-1]
    i = pl.program_id(0)

    def row_copy(t, d):
        return pltpu.make_async_copy(h_ref.at[pl.ds(t, 1)], xr_ref.at[pl.ds(d, 1)], sem.at[0])

    def start(t, carry):
        for k in range(TOP_K):
            row_copy(t, dest_ref[(i * td + t) * TOP_K + k]).start()
        return carry

    def wait(t, carry):
        for k in range(TOP_K):
            row_copy(0, 0).wait()
        return carry

    lax.fori_loop(0, td, start, 0)
    lax.fori_loop(0, td, wait, 0)


def _dispatch(dest_flat, h, xr, n_rows):
    rows, d = h.shape
    td = _tile(rows, DISPATCH_TD)
    in_specs = [pl.BlockSpec((td, d), lambda i, dest: (i, 0))]
    args = [dest_flat, h]
    aliases = {}
    if xr is not None:
        in_specs.append(pl.BlockSpec(memory_space=pl.ANY))
        args.append(xr)
        aliases = {2: 0}
    grid_spec = pltpu.PrefetchScalarGridSpec(
        num_scalar_prefetch=1,
        grid=(rows // td,),
        in_specs=in_specs,
        out_specs=pl.BlockSpec(memory_space=pl.ANY),
        scratch_shapes=[pltpu.SemaphoreType.DMA((1,))],
    )
    return pl.pallas_call(
        functools.partial(_dispatch_kernel, td=td),
        grid_spec=grid_spec,
        out_shape=jax.ShapeDtypeStruct((n_rows, d), jnp.float32),
        input_output_aliases=aliases,
        compiler_params=_cparams(("arbitrary",)),
        name="dispatch",
    )(*args)


def _moe_kernel(ge_ref, gsb_ref, gn_ref, xr_hbm, wg_ref, bg_ref, wu_ref, bu_ref, wd_ref, bd_ref, y_hbm,
                xst, xbf, hdn_ref, yst, wgb, wub, wdb, xsem, ysem, *, sb, nsub, nf, ncol):
    g = pl.program_id(0)
    t = pl.program_id(1)
    n = gn_ref[g]
    sb0 = gsb_ref[g]
    fch = wgb.shape[1]
    fcn = wdb.shape[1]

    def x_copy(s):
        return pltpu.make_async_copy(xr_hbm.at[pl.ds((sb0 + s) * sb, sb)], xst.at[s % 2], xsem.at[s % 2])

    def y_copy(s, col, par):
        return pltpu.make_async_copy(yst.at[par, pl.ds(s * sb, sb)],
                                     y_hbm.at[pl.ds((sb0 + s) * sb, sb), pl.ds(col * fcn, fcn)], ysem.at[par])

    def for_active(fn):
        for s in range(nsub):
            @pl.when(s < n)
            def _():
                fn(s)

    def for_active_blocks(fn):
        for s in range(0, nsub, 2):
            @pl.when(s + 1 < n)
            def _():
                fn(s * sb, 2 * sb)

            @pl.when(s + 1 == n)
            def _():
                fn(s * sb, sb)

    g_next = jnp.minimum(g + 1, pl.num_programs(0) - 1)
    n_next = jnp.where(g + 1 < pl.num_programs(0), gn_ref[g_next], 0)
    sb0_next = gsb_ref[g_next]
    per_step = -(-nsub // ncol)

    def next_copy(s, j):
        return pltpu.make_async_copy(xr_hbm.at[pl.ds((sb0_next + s) * sb, sb)], xst.at[j], xsem.at[j])

    @pl.when(n > 0)
    def _group():
        @pl.when((t == 0) & (g == 0))
        def _load():
            x_copy(0).start()
            for s in range(nsub):
                @pl.when(s < n)
                def _():
                    if s + 1 < nsub:
                        @pl.when(s + 1 < n)
                        def _():
                            x_copy(s + 1).start()
                    x_copy(s).wait()
                    xbf[s * sb:(s + 1) * sb, :] = xst[s % 2].astype(jnp.bfloat16)

        @pl.when(t < nf)
        def _gate_up():
            wgb[...] = wg_ref[0].astype(jnp.bfloat16)
            wub[...] = wu_ref[0].astype(jnp.bfloat16)
            bg = bg_ref[0]
            bu = bu_ref[0]

            def fn(r0, nr):
                for c in range(nr // sb):
                    rs = slice(r0 + c * sb, r0 + (c + 1) * sb)
                    x = xbf[rs, :]
                    gg = jnp.dot(x, wgb[...], preferred_element_type=jnp.float32) + bg
                    uu = jnp.dot(x, wub[...], preferred_element_type=jnp.float32) + bu
                    gg = jnp.minimum(gg, SWIGLU_LIMIT)
                    uu = jnp.clip(uu, -SWIGLU_LIMIT, SWIGLU_LIMIT)
                    hdn = gg * jax.nn.sigmoid(SWIGLU_ALPHA * gg) * (uu + 1.0)
                    hdn_ref[t, rs, :] = hdn.astype(jnp.bfloat16)
            for_active_blocks(fn)

        @pl.when(t >= nf)
        def _down():
            col = t - nf
            par = col % 2
            for j in range(per_step):
                @pl.when(col * per_step + j < n_next)
                def _():
                    next_copy(col * per_step + j, j).start()
            wdb[...] = wd_ref[0].astype(jnp.bfloat16)
            bd = bd_ref[0]

            @pl.when(col >= 2)
            def _():
                for_active(lambda s: y_copy(s, col, par).wait())

            def fn(r0, nr):
                y = bd
                for ff in range(nf):
                    y = y + jnp.dot(hdn_ref[ff, r0:r0 + nr, :], wdb[ff * fch:(ff + 1) * fch, :],
                                    preferred_element_type=jnp.float32)
                yst[par, r0:r0 + nr, :] = y
            for_active_blocks(fn)

            for_active(lambda s: y_copy(s, col, par).start())

            for j in range(per_step):
                @pl.when(col * per_step + j < n_next)
                def _():
                    s_next = col * per_step + j
                    next_copy(s_next, j).wait()
                    xbf[pl.ds(pl.multiple_of(s_next * sb, sb), sb), :] = xst[j].astype(jnp.bfloat16)

            @pl.when(col == ncol - 1)
            def _drain():
                def both(s):
                    y_copy(s, col, 1 - par).wait()
                    y_copy(s, col, par).wait()
                for_active(both)


def _moe_experts(g_expert, g_sb0, g_nsb, xr, w_gate, b_gate, w_up, b_up, w_down, b_down):
    n_rows, d = xr.shape
    ne, _, dff = w_gate.shape
    fch = _tile(dff, MOE_F)
    nf = dff // fch
    fcn = _tile(d, MOE_FN)
    ncol = d // fcn
    assert ncol >= 2
    assert MOE_NSUB <= 2 * ncol
    n_groups = g_expert.shape[0]
    rg = MOE_NSUB * MOE_SB

    def gu_idx(g, t, ge, gsb, gn):
        return jnp.where(gn[g] > 0, jnp.minimum(t, nf - 1), nf - 1)

    def dn_idx(g, t, ge, gsb, gn):
        return jnp.where(gn[g] > 0, jnp.maximum(t - nf, 0), ncol - 1)

    grid_spec = pltpu.PrefetchScalarGridSpec(
        num_scalar_prefetch=3,
        grid=(n_groups, nf + ncol),
        in_specs=[
            pl.BlockSpec(memory_space=pl.ANY),
            pl.BlockSpec((1, d, fch), lambda g, t, ge, gsb, gn: (ge[g], 0, gu_idx(g, t, ge, gsb, gn))),
            pl.BlockSpec((1, 1, fch), lambda g, t, ge, gsb, gn: (ge[g], 0, gu_idx(g, t, ge, gsb, gn))),
            pl.BlockSpec((1, d, fch), lambda g, t, ge, gsb, gn: (ge[g], 0, gu_idx(g, t, ge, gsb, gn))),
            pl.BlockSpec((1, 1, fch), lambda g, t, ge, gsb, gn: (ge[g], 0, gu_idx(g, t, ge, gsb, gn))),
            pl.BlockSpec((1, dff, fcn), lambda g, t, ge, gsb, gn: (ge[g], 0, dn_idx(g, t, ge, gsb, gn))),
            pl.BlockSpec((1, 1, fcn), lambda g, t, ge, gsb, gn: (ge[g], 0, dn_idx(g, t, ge, gsb, gn))),
        ],
        out_specs=pl.BlockSpec(memory_space=pl.ANY),
        scratch_shapes=[
            pltpu.VMEM((2, MOE_SB, d), jnp.float32),
            pltpu.VMEM((rg, d), jnp.bfloat16),
            pltpu.VMEM((nf, rg, fch), jnp.bfloat16),
            pltpu.VMEM((2, rg, fcn), jnp.float32),
            pltpu.VMEM((d, fch), jnp.bfloat16),
            pltpu.VMEM((d, fch), jnp.bfloat16),
            pltpu.VMEM((dff, fcn), jnp.bfloat16),
            pltpu.SemaphoreType.DMA((2,)),
            pltpu.SemaphoreType.DMA((2,)),
        ],
    )
    return pl.pallas_call(
        functools.partial(_moe_kernel, sb=MOE_SB, nsub=MOE_NSUB, nf=nf, ncol=ncol),
        grid_spec=grid_spec,
        out_shape=jax.ShapeDtypeStruct((n_rows, d), jnp.float32),
        compiler_params=_cparams(("arbitrary", "arbitrary")),
        name="moe_experts",
    )(g_expert, g_sb0, g_nsb, xr, w_gate, b_gate.reshape(ne, 1, dff), w_up, b_up.reshape(ne, 1, dff),
      w_down, b_down.reshape(ne, 1, d))


def _combine_kernel(dest_ref, h_ref, gate_ref, g_ref, b_ref, y_hbm, o_ref, gbuf, sem, *, td, alpha):
    i = pl.program_id(0)
    n = pl.num_programs(0)
    slot = i % 2

    def row_copy(step, t, k, sl):
        d = dest_ref[(step * td + t) * TOP_K + k]
        return pltpu.make_async_copy(y_hbm.at[pl.ds(d, 1)], gbuf.at[sl, k, pl.ds(t, 1)], sem.at[sl])

    def issue(step, sl):
        def body(t, carry):
            for k in range(TOP_K):
                row_copy(step, t, k, sl).start()
            return carry
        lax.fori_loop(0, td, body, 0)

    @pl.when(i == 0)
    def _():
        issue(0, 0)

    @pl.when(i + 1 < n)
    def _():
        issue(i + 1, 1 - slot)

    def wait(t, carry):
        for k in range(TOP_K):
            pltpu.make_async_copy(y_hbm.at[pl.ds(0, 1)], gbuf.at[slot, k, pl.ds(0, 1)], sem.at[slot]).wait()
        return carry
    lax.fori_loop(0, td, wait, 0)

    gates = gate_ref[...]
    moe = gates[:, 0:1] * gbuf[slot, 0]
    for k in range(1, TOP_K):
        moe = moe + gates[:, k:k + 1] * gbuf[slot, k]
    o_ref[...] = _layer_norm(alpha * h_ref[...] + moe, g_ref[...], b_ref[...])


def _combine(dest_flat, h, gates, ln_g, ln_b, yr, alpha):
    rows, d = h.shape
    td = _tile(rows, DISPATCH_TD)
    grid_spec = pltpu.PrefetchScalarGridSpec(
        num_scalar_prefetch=1,
        grid=(rows // td,),
        in_specs=[
            pl.BlockSpec((td, d), lambda i, dest: (i, 0)),
            pl.BlockSpec((td, TOP_K), lambda i, dest: (i, 0)),
            pl.BlockSpec((1, d), lambda i, dest: (0, 0)),
            pl.BlockSpec((1, d), lambda i, dest: (0, 0)),
            pl.BlockSpec(memory_space=pl.ANY),
        ],
        out_specs=pl.BlockSpec((td, d), lambda i, dest: (i, 0)),
        scratch_shapes=[pltpu.VMEM((2, TOP_K, td, d), jnp.float32), pltpu.SemaphoreType.DMA((2,))],
    )
    return pl.pallas_call(
        functools.partial(_combine_kernel, td=td, alpha=alpha),
        grid_spec=grid_spec,
        out_shape=jax.ShapeDtypeStruct((rows, d), jnp.float32),
        compiler_params=_cparams(("arbitrary",)),
        name="combine",
    )(dest_flat, h, gates, ln_g, ln_b, yr)


def _rope_tables(first, count):
    half = QK_DIM // 2
    log_theta = jnp.log(jnp.asarray(ROPE_THETA, jnp.float32))
    inv = jnp.exp(-log_theta * jnp.arange(half, dtype=jnp.float32) * (2.0 / QK_DIM))
    pos = first + jnp.arange(count)
    ang = pos.astype(jnp.float32)[:, None] * inv[None, :]
    cos, sin = jnp.cos(ang), jnp.sin(ang)
    reps = LANES // QK_DIM
    return jnp.tile(jnp.concatenate([cos, cos], axis=-1), (1, reps)), \
        jnp.tile(jnp.concatenate([-sin, sin], axis=-1), (1, reps))


def _shifted_taps(u, state):
    b, s, c = u.shape
    full = jnp.concatenate([state, u], axis=1)
    taps = [full[:, j:j + s].reshape(b * s, c) for j in range(CONV_K)]
    return taps, full[:, s:]


def _routing_tables(counts, n_groups):
    nsb = (counts + MOE_SB - 1) // MOE_SB
    sb_end = jnp.cumsum(nsb)
    sb_off = sb_end - nsb
    ng = (nsb + MOE_NSUB - 1) // MOE_NSUB
    g_end = jnp.cumsum(ng)
    g_off = g_end - ng
    gi = jnp.arange(n_groups, dtype=jnp.int32)
    last = jnp.maximum(g_end[-1] - 1, 0)
    ge = jnp.sum((g_end[None, :] <= jnp.minimum(gi, last)[:, None]).astype(jnp.int32), axis=1)
    ge = jnp.minimum(ge, counts.shape[0] - 1)
    local = gi - g_off[ge]
    g_sb0 = sb_off[ge] + local * MOE_NSUB
    g_nsb = jnp.where(gi < g_end[-1], jnp.clip(nsb[ge] - local * MOE_NSUB, 0, MOE_NSUB), 0)
    return sb_off * MOE_SB, ge, g_sb0.astype(jnp.int32), g_nsb.astype(jnp.int32)


def kernel(x_prompt, x_sample, cache_k, cache_v, state_conv, page_table, w_in, conv_w, lambda_q, lambda_k, subln_g, w_out, ln1_g, ln1_b, w_router, b_router, w_gate, b_gate, w_up, b_up, w_down, b_down, ln2_g, ln2_b):
    bp, sp, d = x_prompt.shape
    bs, ts, _ = x_sample.shape
    depth, n_pool = cache_k.shape[0], cache_k.shape[1]
    nh = cache_k.shape[3]
    att_dim = nh * V_DIM
    conv_dim = conv_w.shape[-1]
    ne = w_router.shape[-1]
    n_past = page_table.shape[1] * PAGE_SIZE
    alpha = (2.0 * depth) ** 0.25
    assert ts == 4 and cache_k.shape[2] == PAGE_SIZE and w_in.shape[-1] == 3 * att_dim + 3 * conv_dim

    cos_p, sin_p = _rope_tables(0, sp)
    cos_s, sin_s = _rope_tables(n_past, ts)
    cos_s, sin_s = jnp.tile(cos_s, (bs, 1)), jnp.tile(sin_s, (bs, 1))
    cache_k2 = cache_k.reshape(depth * n_pool, PAGE_SIZE * nh, V_DIM)
    cache_v2 = cache_v.reshape(depth * n_pool, PAGE_SIZE * nh, V_DIM)

    n_tok = bp * sp + bs * ts
    n_rows = (n_tok * TOP_K // MOE_SB + ne) * MOE_SB
    n_groups = ne + max(n_rows // MOE_SB - ne, 0) // MOE_NSUB + 1

    xp = x_prompt.reshape(bp * sp, d)
    xs = x_sample.reshape(bs * ts, d)
    outs = {k: [] for k in ('kp', 'vp', 'cp', 'ks', 'vs', 'cs')}
    for l in range(depth):
        lam_init = 0.8 - 0.6 * math.exp(-0.3 * l)
        w_in_bf = w_in[l].astype(jnp.bfloat16)
        w_out_bf = w_out[l].astype(jnp.bfloat16)
        lq, lk, sg = lambda_q[l], lambda_k[l], subln_g[l].reshape(1, V_DIM)
        g1, b1 = ln1_g[l].reshape(1, d), ln1_b[l].reshape(1, d)
        g2, b2 = ln2_g[l].reshape(1, d), ln2_b[l].reshape(1, d)
        br = b_router[l].reshape(1, ne)

        xp_bf = xp.astype(jnp.bfloat16)
        q_bf, k_f, k_bf, v_f, _, vt_bf = _qkv_proj(xp_bf, w_in_bf, cos_p, sin_p, att_dim)
        u_p, gb_p = _conv_proj(xp_bf, w_in_bf, att_dim, conv_dim)
        att_p = _flash_prompt(q_bf, k_bf, vt_bf, lq, lk, sg.reshape(V_DIM, 1), bp, sp, lam_init)
        taps, cst_p = _shifted_taps(u_p.reshape(bp, sp, conv_dim), jnp.zeros((bp, CONV_K - 1, conv_dim), u_p.dtype))
        conv_p = _conv_gate(*taps, gb_p, conv_w[l])
        outs['kp'].append(k_f.reshape(bp, sp, nh, V_DIM))
        outs['vp'].append(v_f.reshape(bp, sp, nh, V_DIM))
        outs['cp'].append(cst_p)

        xs_bf = xs.astype(jnp.bfloat16)
        qs_bf, ks_f, ks_bf, vs_f, vs_bf, _ = _qkv_proj(xs_bf, w_in_bf, cos_s, sin_s, att_dim)
        u_s, gb_s = _conv_proj(xs_bf, w_in_bf, att_dim, conv_dim)
        q5 = qs_bf.reshape(bs, ts, nh, 2, QK_DIM)
        eye = jnp.eye(2, dtype=q5.dtype)
        qall = jnp.einsum('bthmd,mn->bhmtnd', q5, eye).reshape(bs, nh * 2 * ts, V_DIM)
        att_s = _sample_attention(qall, ks_bf.reshape(bs, ts * nh, V_DIM), vs_bf.reshape(bs, ts * nh, V_DIM),
                                  cache_k2, cache_v2, page_table, l, n_pool, lq, lk, sg, ts, lam_init)
        att_s = att_s.reshape(bs, nh, 2 * ts, V_DIM)[:, :, :ts]
        att_s = jnp.transpose(att_s, (0, 2, 1, 3)).reshape(bs * ts, att_dim).astype(jnp.bfloat16)
        taps, cst_s = _shifted_taps(u_s.reshape(bs, ts, conv_dim), state_conv[l])
        conv_s = _conv_gate(*taps, gb_s, conv_w[l])
        outs['ks'].append(ks_f.reshape(bs, ts, nh, V_DIM))
        outs['vs'].append(vs_f.reshape(bs, ts, nh, V_DIM))
        outs['cs'].append(cst_s)

        cnt0 = jnp.zeros((1, ne), jnp.float32)
        h_p, ids_p, rank_p, gate_p, cnt_p = _out_proj_route(att_p, conv_p, xp, w_out_bf, g1, b1, w_router[l], br, cnt0, alpha)
        h_s, ids_s, rank_s, gate_s, cnt_all = _out_proj_route(att_s, conv_s, xs, w_out_bf, g1, b1, w_router[l], br, cnt_p, alpha)

        counts = cnt_all.reshape(ne).astype(jnp.int32)
        row_off, g_expert, g_sb0, g_nsb = _routing_tables(counts, n_groups)
        dest_p = (row_off[ids_p.reshape(-1)] + rank_p.reshape(-1)).astype(jnp.int32)
        dest_s = (row_off[ids_s.reshape(-1)] + rank_s.reshape(-1)).astype(jnp.int32)

        xr = _dispatch(dest_p, h_p, None, n_rows)
        xr = _dispatch(dest_s, h_s, xr, n_rows)
        yr = _moe_experts(g_expert, g_sb0, g_nsb, xr, w_gate[l], b_gate[l], w_up[l], b_up[l], w_down[l], b_down[l])
        xp = _combine(dest_p, h_p, gate_p, g2, b2, yr, alpha)
        xs = _combine(dest_s, h_s, gate_s, g2, b2, yr, alpha)

    return (xp.reshape(bp, sp, d), xs.reshape(bs, ts, d),
            jnp.stack(outs['kp']), jnp.stack(outs['vp']), jnp.stack(outs['cp']),
            jnp.stack(outs['ks']), jnp.stack(outs['vs']), jnp.stack(outs['cs']))
```

```python
import functools
import math

import jax
import jax.numpy as jnp
from jax import lax
from jax.experimental import pallas as pl
from jax.experimental.pallas import tpu as pltpu

QK_DIM = 64
V_DIM = 2 * QK_DIM
CONV_K = 3
PAGE_SIZE = 128
ROPE_THETA = 10000.0
SUBLN_EPS = 1e-5
LN_EPS = 1e-5
TOP_K = 4
SWIGLU_ALPHA = 1.702
SWIGLU_LIMIT = 7.0

LANES = 128
NEG_BIG = -1e30
Q_SCALE = QK_DIM ** -0.5 * math.log2(math.e)

FLASH_T = 512
FLASH_HEADS = 8
FLASH_CW = 256

VMEM_LIMIT = 56 * 1024 * 1024

MOE_SB = 256
MOE_NSUB = 8
MOE_F = 256
MOE_FN = 512
DISPATCH_TD = 128
SAMPLE_PAGES = 16
SAMPLE_SUB_PAGES = 8


def _cparams(sem):
    return pltpu.CompilerParams(dimension_semantics=sem, vmem_limit_bytes=VMEM_LIMIT)


def _tile(n, pref):
    t = min(n, pref)
    while n % t:
        t //= 2
    return t


def _rope_chunks(a, cos, sin_signed):
    lane = lax.broadcasted_iota(jnp.int32, (a.shape[0], LANES), 1)
    first = (lane % QK_DIM) < (QK_DIM // 2)
    outs = []
    for c in range(a.shape[1] // LANES):
        ac = a[:, c * LANES:(c + 1) * LANES]
        partner = jnp.where(first, pltpu.roll(ac, LANES - QK_DIM // 2, 1), pltpu.roll(ac, QK_DIM // 2, 1))
        outs.append(ac * cos + partner * sin_signed)
    return outs


def _qkv_kernel(x_ref, w_ref, cos_ref, sin_ref, qb_ref, kf_ref, kb_ref, vf_ref, vb_ref, vt_ref, *, nsec, rc):
    j = pl.program_id(1)
    tm = x_ref.shape[0]

    def chunks(epilogue):
        for r in range(tm // rc):
            rs = slice(r * rc, (r + 1) * rc)
            epilogue(rs, jnp.dot(x_ref[rs, :], w_ref[...], preferred_element_type=jnp.float32))

    @pl.when(j < nsec)
    def _q():
        def epilogue(rs, acc):
            for c, o in enumerate(_rope_chunks(acc, cos_ref[rs, :], sin_ref[rs, :])):
                qb_ref[rs, c * LANES:(c + 1) * LANES] = (o * Q_SCALE).astype(qb_ref.dtype)
        chunks(epilogue)

    @pl.when((j >= nsec) & (j < 2 * nsec))
    def _k():
        def epilogue(rs, acc):
            for c, o in enumerate(_rope_chunks(acc, cos_ref[rs, :], sin_ref[rs, :])):
                kf_ref[rs, c * LANES:(c + 1) * LANES] = o
                kb_ref[rs, c * LANES:(c + 1) * LANES] = o.astype(kb_ref.dtype)
        chunks(epilogue)

    @pl.when(j >= 2 * nsec)
    def _v():
        def epilogue(rs, acc):
            vf_ref[rs, :] = acc
            vb_ref[rs, :] = acc.astype(vb_ref.dtype)
            vt_ref[:, rs] = acc.T.astype(vt_ref.dtype)
        chunks(epilogue)


def _qkv_proj(x_bf, w_bf, cos, sin_signed, att_dim):
    rows, d = x_bf.shape
    tm = _tile(rows, 1024)
    tn = _tile(att_dim, 512)
    nsec = att_dim // tn
    table_blocks = cos.shape[0] // tm
    assert cos.shape[0] % tm == 0
    sec = lambda s: (lambda i, j: (i, jnp.clip(j - s * nsec, 0, nsec - 1)))
    f32 = jax.ShapeDtypeStruct((rows, att_dim), jnp.float32)
    b16 = jax.ShapeDtypeStruct((rows, att_dim), jnp.bfloat16)
    return pl.pallas_call(
        functools.partial(_qkv_kernel, nsec=nsec, rc=_tile(tm, 256)),
        grid=(rows // tm, 3 * nsec),
        in_specs=[
            pl.BlockSpec((tm, d), lambda i, j: (i, 0)),
            pl.BlockSpec((d, tn), lambda i, j: (0, j)),
            pl.BlockSpec((tm, LANES), lambda i, j: (i % table_blocks, 0)),
            pl.BlockSpec((tm, LANES), lambda i, j: (i % table_blocks, 0)),
        ],
        out_specs=[
            pl.BlockSpec((tm, tn), sec(0)),
            pl.BlockSpec((tm, tn), sec(1)),
            pl.BlockSpec((tm, tn), sec(1)),
            pl.BlockSpec((tm, tn), sec(2)),
            pl.BlockSpec((tm, tn), sec(2)),
            pl.BlockSpec((tn, tm), lambda i, j: (jnp.clip(j - 2 * nsec, 0, nsec - 1), i)),
        ],
        out_shape=[b16, f32, b16, f32, b16, jax.ShapeDtypeStruct((att_dim, rows), jnp.bfloat16)],
        compiler_params=_cparams(("parallel", "arbitrary")),
        name="qkv_proj",
    )(x_bf, w_bf, cos, sin_signed)


def _convproj_kernel(x_ref, wh_ref, wb_ref, wc_ref, u_ref, gb_ref):
    x = x_ref[...]
    h = jnp.dot(x, wh_ref[...], preferred_element_type=jnp.float32)
    gc = jnp.dot(x, wc_ref[...], preferred_element_type=jnp.float32)
    u_ref[...] = gc * h
    gb_ref[...] = jnp.dot(x, wb_ref[...], preferred_element_type=jnp.float32)


def _conv_proj(x_bf, w_bf, att_dim, conv_dim):
    rows, d = x_bf.shape
    tm = _tile(rows, 1024)
    tn = _tile(conv_dim, 512)
    base = 3 * att_dim // tn
    nsec = conv_dim // tn
    wspec = lambda s: pl.BlockSpec((d, tn), lambda i, j: (0, base + s * nsec + j))
    f32 = jax.ShapeDtypeStruct((rows, conv_dim), jnp.float32)
    return pl.pallas_call(
        _convproj_kernel,
        grid=(rows // tm, nsec),
        in_specs=[pl.BlockSpec((tm, d), lambda i, j: (i, 0)), wspec(0), wspec(1), wspec(2)],
        out_specs=[pl.BlockSpec((tm, tn), lambda i, j: (i, j))] * 2,
        out_shape=[f32, f32],
        compiler_params=_cparams(("parallel", "arbitrary")),
        name="conv_proj",
    )(x_bf, w_bf, w_bf, w_bf)


def _conv_gate_kernel(s0_ref, s1_ref, s2_ref, gb_ref, w_ref, o_ref):
    w = w_ref[...]
    y = s0_ref[...] * w[0:1] + s1_ref[...] * w[1:2] + s2_ref[...] * w[2:3]
    o_ref[...] = (gb_ref[...] * y).astype(o_ref.dtype)


def _conv_gate(s0, s1, s2, gb, conv_w):
    rows, c = gb.shape
    tm = _tile(rows, 512)
    blk = pl.BlockSpec((tm, c), lambda i: (i, 0))
    return pl.pallas_call(
        _conv_gate_kernel,
        grid=(rows // tm,),
        in_specs=[blk, blk, blk, blk, pl.BlockSpec((CONV_K, c), lambda i: (0, 0))],
        out_specs=blk,
        out_shape=jax.ShapeDtypeStruct((rows, c), jnp.bfloat16),
        compiler_params=_cparams(("parallel",)),
        name="conv_gate",
    )(s0, s1, s2, gb, conv_w)


def _lambda_full(lq_ref, lk_ref, lam_init):
    prod = lq_ref[...] * lk_ref[...]
    e = jnp.exp(jnp.sum(prod, axis=-1, keepdims=True))
    return e[0:1] - e[1:2] + lam_init


def _subln(o, g, lam_init):
    ms = jnp.mean(o * o, axis=-1, keepdims=True)
    return o * lax.rsqrt(ms + SUBLN_EPS) * g * (1.0 - lam_init)


def _flash_kernel(iq_ref, ik_ref, q_ref, k_ref, vt_ref, lq_ref, lk_ref, g_ref, o_ref, qq_ref, m_ref, l_ref, acc_ref,
                  *, tq, hp, cw, lam_init):
    pair = pl.program_id(2)
    iq = iq_ref[pair]
    ik = ik_ref[pair]
    nc = 2 * tq // cw

    @pl.when(ik == 0)
    def _init():
        for h in range(hp):
            q = q_ref[:, h * V_DIM:(h + 1) * V_DIM]
            lane = lax.broadcasted_iota(jnp.int32, q.shape, 1)
            zero = jnp.zeros_like(q)
            qq_ref[h, 0:tq] = jnp.where(lane < QK_DIM, q, zero)
            qq_ref[h, tq:2 * tq] = jnp.where(lane >= QK_DIM, q, zero)
        m_ref[...] = jnp.full(m_ref.shape, NEG_BIG, jnp.float32)
        l_ref[...] = jnp.zeros(l_ref.shape, jnp.float32)
        acc_ref[...] = jnp.zeros(acc_ref.shape, jnp.float32)

    def step(masked):
        scores = []
        for h in range(hp):
            kk = k_ref[:, h * V_DIM:(h + 1) * V_DIM]
            for c in range(nc):
                s = lax.dot_general(kk, qq_ref[h, c * cw:(c + 1) * cw, :], (((1,), (1,)), ((), ())),
                                    preferred_element_type=jnp.float32)
                if masked:
                    key = lax.broadcasted_iota(jnp.int32, s.shape, 0)
                    qry = lax.broadcasted_iota(jnp.int32, s.shape, 1) + ((c * cw) % tq)
                    s = jnp.where(key <= qry, s, NEG_BIG)
                scores.append(s)
        for h in range(hp):
            vt = vt_ref[h * V_DIM:(h + 1) * V_DIM, :]
            m_all = m_ref[h]
            l_all = l_ref[h]
            m_out, l_out = [], []
            for c in range(nc):
                cs = slice(c * cw, (c + 1) * cw)
                s = scores[h * nc + c]
                m_prev = m_all[:, cs]
                m_new = jnp.maximum(m_prev, jnp.max(s, axis=0, keepdims=True))
                alpha = jnp.exp2(m_prev - m_new)
                p = jnp.exp2(s - m_new)
                l_out.append(alpha * l_all[:, cs] + jnp.sum(p, axis=0, keepdims=True))
                m_out.append(m_new)
                acc_ref[h, :, cs] = alpha * acc_ref[h, :, cs] + jnp.dot(vt, p.astype(vt.dtype),
                                                                        preferred_element_type=jnp.float32)
            m_ref[h] = jnp.concatenate(m_out, axis=1)
            l_ref[h] = jnp.concatenate(l_out, axis=1)

    @pl.when(ik < iq)
    def _off_diag():
        step(False)

    @pl.when(ik == iq)
    def _diag():
        step(True)
        lam = _lambda_full(lq_ref, lk_ref, lam_init)
        for h in range(hp):
            o = acc_ref[h] / l_ref[h]
            o = o[:, 0:tq] - lam * o[:, tq:2 * tq]
            ms = jnp.mean(o * o, axis=0, keepdims=True)
            o = o * lax.rsqrt(ms + SUBLN_EPS) * g_ref[...] * (1.0 - lam_init)
            o_ref[:, h * V_DIM:(h + 1) * V_DIM] = o.T.astype(o_ref.dtype)


def _flash_prompt(q_bf, k_bf, vt_bf, lam_q, lam_k, subln_g_col, batch, seq, lam_init):
    rows, att_dim = q_bf.shape
    nh = att_dim // V_DIM
    hp = _tile(nh, FLASH_HEADS)
    tq = _tile(seq, FLASH_T)
    cw = _tile(2 * tq, FLASH_CW)
    nq = seq // tq
    pairs = [(i, k) for i in range(nq) for k in range(i + 1)]
    iq_tab = jnp.asarray([p[0] for p in pairs], jnp.int32)
    ik_tab = jnp.asarray([p[1] for p in pairs], jnp.int32)
    qspec = pl.BlockSpec((tq, hp * V_DIM), lambda b, h, p, iqt, ikt: (b * nq + iqt[p], h))
    kspec = pl.BlockSpec((tq, hp * V_DIM), lambda b, h, p, iqt, ikt: (b * nq + ikt[p], h))
    vspec = pl.BlockSpec((hp * V_DIM, tq), lambda b, h, p, iqt, ikt: (h, b * nq + ikt[p]))
    small = lambda shape: pl.BlockSpec(shape, lambda b, h, p, iqt, ikt: (0, 0))
    grid_spec = pltpu.PrefetchScalarGridSpec(
        num_scalar_prefetch=2,
        grid=(batch, nh // hp, len(pairs)),
        in_specs=[qspec, kspec, vspec, small((2, QK_DIM)), small((2, QK_DIM)), small((V_DIM, 1))],
        out_specs=qspec,
        scratch_shapes=[
            pltpu.VMEM((hp, 2 * tq, V_DIM), jnp.bfloat16),
            pltpu.VMEM((hp, 1, 2 * tq), jnp.float32),
            pltpu.VMEM((hp, 1, 2 * tq), jnp.float32),
            pltpu.VMEM((hp, V_DIM, 2 * tq), jnp.float32),
        ],
    )
    return pl.pallas_call(
        functools.partial(_flash_kernel, tq=tq, hp=hp, cw=cw, lam_init=lam_init),
        grid_spec=grid_spec,
        out_shape=jax.ShapeDtypeStruct((rows, att_dim), jnp.bfloat16),
        compiler_params=_cparams(("parallel", "parallel", "arbitrary")),
        name="flash_prompt",
    )(iq_tab, ik_tab, q_bf, k_bf, vt_bf, lam_q, lam_k, subln_g_col)


def _sample_attn_kernel(pt_ref, qall_ref, knew_ref, vnew_ref, bias_ref, bias_new_ref, lq_ref, lk_ref, g_ref,
                        ck_hbm, cv_hbm, o_ref, kbuf, vbuf, ksem, vsem, m_ref, l_ref, acc_ref,
                        *, n_chunks, pages, page_rows, pool_base, lam_init):
    b = pl.program_id(0)
    c = pl.program_id(1)
    step = b * n_chunks + c
    n_steps = pl.num_programs(0) * n_chunks
    slot = step % 2

    def copies(bb, cc, sl):
        out = []
        for j in range(pages):
            page = pt_ref[bb, cc * pages + j] + pool_base
            dst = pl.ds(j * page_rows, page_rows)
            out.append(pltpu.make_async_copy(ck_hbm.at[page], kbuf.at[sl, dst], ksem.at[sl]))
            out.append(pltpu.make_async_copy(cv_hbm.at[page], vbuf.at[sl, dst], vsem.at[sl]))
        return out

    @pl.when(step == 0)
    def _prime():
        for cp in copies(b, c, slot):
            cp.start()

    @pl.when(step + 1 < n_steps)
    def _prefetch():
        nxt = step + 1
        for cp in copies(nxt // n_chunks, nxt % n_chunks, 1 - slot):
            cp.start()

    @pl.when(c == 0)
    def _init():
        m_ref[...] = jnp.full(m_ref.shape, NEG_BIG, jnp.float32)
        l_ref[...] = jnp.zeros(l_ref.shape, jnp.float32)
        acc_ref[...] = jnp.zeros(acc_ref.shape, jnp.float32)

    for cp in copies(b, c, slot):
        cp.wait()

    qall = qall_ref[0]

    def update(kk, vv, bias):
        s = lax.dot_general(qall, kk, (((1,), (1,)), ((), ())), preferred_element_type=jnp.float32) + bias
        m_prev = m_ref[...]
        m_new = jnp.maximum(m_prev, jnp.max(s, axis=-1, keepdims=True))
        alpha = jnp.exp2(m_prev - m_new)
        p = jnp.exp2(s - m_new)
        l_ref[...] = alpha * l_ref[...] + jnp.sum(p, axis=-1, keepdims=True)
        acc_ref[...] = alpha * acc_ref[...] + jnp.dot(p.astype(vv.dtype), vv, preferred_element_type=jnp.float32)
        m_ref[...] = m_new

    sub = bias_ref.shape[1]
    for u in range(pages * page_rows // sub):
        rows = pl.ds(u * sub, sub)
        update(kbuf[slot, rows, :].astype(jnp.bfloat16), vbuf[slot, rows, :].astype(jnp.bfloat16), bias_ref[...])

    @pl.when(c == n_chunks - 1)
    def _finish():
        update(knew_ref[0], vnew_ref[0], bias_new_ref[...])
        o = acc_ref[...] / l_ref[...]
        lam = _lambda_full(lq_ref, lk_ref, lam_init)
        g = g_ref[...]
        for h in range(o.shape[0] // 8):
            oh = o[h * 8:(h + 1) * 8]
            oh = oh - lam * pltpu.roll(oh, 4, 0)
            o_ref[0, h * 8:(h + 1) * 8, :] = _subln(oh, g, lam_init)


def _sample_attention(qall, knew, vnew, cache_k2, cache_v2, page_table, layer, n_pool, lam_q, lam_k, subln_g,
                      dec_seq, lam_init):
    nb, nrow, _ = qall.shape
    nh = nrow // 8
    n_pages = page_table.shape[1]
    pages = _tile(n_pages, SAMPLE_PAGES)
    n_chunks = n_pages // pages
    page_rows = PAGE_SIZE * nh
    ncols = pages * page_rows
    sub_cols = _tile(pages, SAMPLE_SUB_PAGES) * page_rows
    row = jnp.arange(nrow)
    col = jnp.arange(sub_cols)
    bias = jnp.where((col[None, :] % nh) == (row[:, None] // 8), 0.0, NEG_BIG).astype(jnp.float32)
    coln = jnp.arange(dec_seq * nh)
    ok = ((coln[None, :] % nh) == (row[:, None] // 8)) & ((coln[None, :] // nh) <= (row[:, None] % 4))
    bias_new = jnp.where(ok, 0.0, NEG_BIG).astype(jnp.float32)
    const = lambda shape: pl.BlockSpec(shape, lambda b, c, pt: (0,) * len(shape))
    per_seq = lambda shape: pl.BlockSpec(shape, lambda b, c, pt: (b,) + (0,) * (len(shape) - 1))
    grid_spec = pltpu.PrefetchScalarGridSpec(
        num_scalar_prefetch=1,
        grid=(nb, n_chunks),
        in_specs=[
            per_seq((1, nrow, V_DIM)),
            per_seq((1, dec_seq * nh, V_DIM)),
            per_seq((1, dec_seq * nh, V_DIM)),
            const((nrow, sub_cols)),
            const((nrow, dec_seq * nh)),
            const((2, QK_DIM)),
            const((2, QK_DIM)),
            const((1, V_DIM)),
            pl.BlockSpec(memory_space=pl.ANY),
            pl.BlockSpec(memory_space=pl.ANY),
        ],
        out_specs=per_seq((1, nrow, V_DIM)),
        scratch_shapes=[
            pltpu.VMEM((2, ncols, V_DIM), jnp.float32),
            pltpu.VMEM((2, ncols, V_DIM), jnp.float32),
            pltpu.SemaphoreType.DMA((2,)),
            pltpu.SemaphoreType.DMA((2,)),
            pltpu.VMEM((nrow, 1), jnp.float32),
            pltpu.VMEM((nrow, 1), jnp.float32),
            pltpu.VMEM((nrow, V_DIM), jnp.float32),
        ],
    )
    return pl.pallas_call(
        functools.partial(_sample_attn_kernel, n_chunks=n_chunks, pages=pages, page_rows=page_rows,
                          pool_base=layer * n_pool, lam_init=lam_init),
        grid_spec=grid_spec,
        out_shape=jax.ShapeDtypeStruct((nb, nrow, V_DIM), jnp.float32),
        compiler_params=_cparams(("arbitrary", "arbitrary")),
        name="sample_attn",
    )(page_table, qall, knew, vnew, bias, bias_new, lam_q, lam_k, subln_g, cache_k2, cache_v2)


def _layer_norm(y, g, b):
    mu = jnp.mean(y, axis=-1, keepdims=True)
    yc = y - mu
    var = jnp.mean(yc * yc, axis=-1, keepdims=True)
    return yc * lax.rsqrt(var + LN_EPS) * g + b


def _outproj_kernel(att_ref, conv_ref, x_ref, wa_ref, wc_ref, g_ref, b_ref, wr_ref, br_ref, cnt0_ref,
                    h_ref, ids_ref, rank_ref, gate_ref, cnt_ref, cnt_scr, *, alpha, rc):
    i = pl.program_id(0)

    @pl.when(i == 0)
    def _():
        cnt_scr[...] = cnt0_ref[...]

    wr = wr_ref[...]
    w_hi = wr.astype(jnp.bfloat16)
    w_lo = (wr - w_hi.astype(jnp.float32)).astype(jnp.bfloat16)
    ne = wr.shape[1]
    lane = lax.broadcasted_iota(jnp.int32, (rc, ne), 1).astype(jnp.float32)
    r = lax.broadcasted_iota(jnp.int32, (rc, rc), 0)
    c = lax.broadcasted_iota(jnp.int32, (rc, rc), 1)
    tri = jnp.where(c < r, 1.0, 0.0).astype(jnp.bfloat16)
    cnt = cnt_scr[...]

    for q in range(att_ref.shape[0] // rc):
        rs = slice(q * rc, (q + 1) * rc)
        mixed = jnp.dot(att_ref[rs, :], wa_ref[...], preferred_element_type=jnp.float32)
        mixed = mixed + jnp.dot(conv_ref[rs, :], wc_ref[...], preferred_element_type=jnp.float32)
        h = _layer_norm(alpha * x_ref[rs, :] + mixed, g_ref[...], b_ref[...])
        h_ref[rs, :] = h

        h_hi = h.astype(jnp.bfloat16)
        h_lo = (h - h_hi.astype(jnp.float32)).astype(jnp.bfloat16)
        logits = (jnp.dot(h_hi, w_hi, preferred_element_type=jnp.float32)
                  + jnp.dot(h_hi, w_lo, preferred_element_type=jnp.float32)
                  + jnp.dot(h_lo, w_hi, preferred_element_type=jnp.float32)) + br_ref[...]

        work = logits
        vals, ids, hot = [], [], []
        for _ in range(TOP_K):
            m = jnp.max(work, axis=-1, keepdims=True)
            idx = jnp.min(jnp.where(work == m, lane, float(ne)), axis=-1, keepdims=True)
            sel = lane == idx
            vals.append(m)
            ids.append(idx.astype(jnp.int32))
            hot.append(sel)
            work = jnp.where(sel, -jnp.inf, work)
        exps = [jnp.exp(v - vals[0]) for v in vals]
        denom = exps[0] + exps[1] + exps[2] + exps[3]

        onehot = jnp.where(hot[0] | hot[1] | hot[2] | hot[3], 1.0, 0.0)
        before = jnp.dot(tri, onehot.astype(jnp.bfloat16), preferred_element_type=jnp.float32) + cnt
        for k in range(TOP_K):
            ids_ref[rs, k:k + 1] = ids[k]
            gate_ref[rs, k:k + 1] = exps[k] / denom
            rank = jnp.sum(jnp.where(hot[k], before, 0.0), axis=-1, keepdims=True)
            rank_ref[rs, k:k + 1] = rank.astype(jnp.int32)
        cnt = cnt + jnp.sum(onehot, axis=0, keepdims=True)

    cnt_scr[...] = cnt
    cnt_ref[...] = cnt


def _out_proj_route(att_bf, conv_bf, x, w_out_bf, ln_g, ln_b, w_router, b_router, cnt0, alpha):
    rows, d = x.shape
    ka = att_bf.shape[1]
    kc = conv_bf.shape[1]
    ne = w_router.shape[1]
    tm = _tile(rows, 512)
    rc = _tile(tm, 256)
    assert ka == kc
    row = lambda n: pl.BlockSpec((tm, n), lambda i: (i, 0))
    const = lambda shape, idx=(0, 0): pl.BlockSpec(shape, lambda i: idx)
    i32 = jax.ShapeDtypeStruct((rows, TOP_K), jnp.int32)
    return pl.pallas_call(
        functools.partial(_outproj_kernel, alpha=alpha, rc=rc),
        grid=(rows // tm,),
        in_specs=[row(ka), row(kc), row(d), const((ka, d), (0, 0)), const((kc, d), (1, 0)),
                  const((1, d)), const((1, d)), const((d, ne)), const((1, ne)), const((1, ne))],
        out_specs=[row(d), row(TOP_K), row(TOP_K), row(TOP_K), const((1, ne))],
        out_shape=[jax.ShapeDtypeStruct((rows, d), jnp.float32), i32, i32,
                   jax.ShapeDtypeStruct((rows, TOP_K), jnp.float32),
                   jax.ShapeDtypeStruct((1, ne), jnp.float32)],
        scratch_shapes=[pltpu.VMEM((1, ne), jnp.float32)],
        compiler_params=_cparams(("arbitrary",)),
        name="out_proj_route",
    )(att_bf, conv_bf, x, w_out_bf, w_out_bf, ln_g, ln_b, w_router, b_router, cnt0)


def _dispatch_kernel(dest_ref, h_ref, *rest, td):
    xr_ref, sem = rest[-2], rest[-1]
    i = pl.program_id(0)

    def row_copy(t, d):
        return pltpu.make_async_copy(h_ref.at[pl.ds(t, 1)], xr_ref.at[pl.ds(d, 1)], sem.at[0])

    def start(t, carry):
        for k in range(TOP_K):
            row_copy(t, dest_ref[(i * td + t) * TOP_K + k]).start()
        return carry

    def wait(t, carry):
        for k in range(TOP_K):
            row_copy(0, 0).wait()
        return carry

    lax.fori_loop(0, td, start, 0)
    lax.fori_loop(0, td, wait, 0)


def _dispatch(dest_flat, h, xr, n_rows):
    rows, d = h.shape
    td = _tile(rows, DISPATCH_TD)
    in_specs = [pl.BlockSpec((td, d), lambda i, dest: (i, 0))]
    args = [dest_flat, h]
    aliases = {}
    if xr is not None:
        in_specs.append(pl.BlockSpec(memory_space=pl.ANY))
        args.append(xr)
        aliases = {2: 0}
    grid_spec = pltpu.PrefetchScalarGridSpec(
        num_scalar_prefetch=1,
        grid=(rows // td,),
        in_specs=in_specs,
        out_specs=pl.BlockSpec(memory_space=pl.ANY),
        scratch_shapes=[pltpu.SemaphoreType.DMA((1,))],
    )
    return pl.pallas_call(
        functools.partial(_dispatch_kernel, td=td),
        grid_spec=grid_spec,
        out_shape=jax.ShapeDtypeStruct((n_rows, d), jnp.float32),
        input_output_aliases=aliases,
        compiler_params=_cparams(("arbitrary",)),
        name="dispatch",
    )(*args)


def _moe_kernel(ge_ref, gsb_ref, gn_ref, xr_hbm, wg_ref, bg_ref, wu_ref, bu_ref, wd_ref, bd_ref, y_hbm,
                xst, xbf, hdn_ref, yst, wgb, wub, wdb, xsem, ysem, *, sb, nsub, nf, ncol):
    g = pl.program_id(0)
    t = pl.program_id(1)
    n = gn_ref[g]
    sb0 = gsb_ref[g]
    fch = wgb.shape[1]
    fcn = wdb.shape[1]

    def x_copy(s):
        return pltpu.make_async_copy(xr_hbm.at[pl.ds((sb0 + s) * sb, sb)], xst.at[s % 2], xsem.at[s % 2])

    def y_copy(s, col, par):
        return pltpu.make_async_copy(yst.at[par, pl.ds(s * sb, sb)],
                                     y_hbm.at[pl.ds((sb0 + s) * sb, sb), pl.ds(col * fcn, fcn)], ysem.at[par])

    def for_active(fn):
        for s in range(nsub):
            @pl.when(s < n)
            def _():
                fn(s)

    def for_active_blocks(fn):
        for s in range(0, nsub, 4):
            left = n - s

            @pl.when(left >= 4)
            def _():
                fn(s * sb, 4 * sb)

            @pl.when((left == 2) | (left == 3))
            def _():
                fn(s * sb, 2 * sb)

            @pl.when(left == 3)
            def _():
                fn((s + 2) * sb, sb)

            @pl.when(left == 1)
            def _():
                fn(s * sb, sb)

    g_next = jnp.minimum(g + 1, pl.num_programs(0) - 1)
    n_next = jnp.where(g + 1 < pl.num_programs(0), gn_ref[g_next], 0)
    sb0_next = gsb_ref[g_next]
    per_step = -(-nsub // ncol)

    def next_copy(s, j):
        return pltpu.make_async_copy(xr_hbm.at[pl.ds((sb0_next + s) * sb, sb)], xst.at[j], xsem.at[j])

    @pl.when(n > 0)
    def _group():
        @pl.when((t == 0) & (g == 0))
        def _load():
            x_copy(0).start()
            for s in range(nsub):
                @pl.when(s < n)
                def _():
                    if s + 1 < nsub:
                        @pl.when(s + 1 < n)
                        def _():
                            x_copy(s + 1).start()
                    x_copy(s).wait()
                    xbf[s * sb:(s + 1) * sb, :] = xst[s % 2].astype(jnp.bfloat16)

        @pl.when(t < nf)
        def _gate_up():
            wgb[...] = wg_ref[0].astype(jnp.bfloat16)
            wub[...] = wu_ref[0].astype(jnp.bfloat16)
            bg = bg_ref[0]
            bu = bu_ref[0]

            def fn(r0, nr):
                for c in range(nr // sb):
                    rs = slice(r0 + c * sb, r0 + (c + 1) * sb)
                    x = xbf[rs, :]
                    gg = jnp.dot(x, wgb[...], preferred_element_type=jnp.float32) + bg
                    uu = jnp.dot(x, wub[...], preferred_element_type=jnp.float32) + bu
                    gg = jnp.minimum(gg, SWIGLU_LIMIT)
                    uu = jnp.clip(uu, -SWIGLU_LIMIT, SWIGLU_LIMIT)
                    hdn = gg * jax.nn.sigmoid(SWIGLU_ALPHA * gg) * (uu + 1.0)
                    hdn_ref[t, rs, :] = hdn.astype(jnp.bfloat16)
            for_active_blocks(fn)

        @pl.when(t >= nf)
        def _down():
            col = t - nf
            par = col % 2
            for j in range(per_step):
                @pl.when(col * per_step + j < n_next)
                def _():
                    next_copy(col * per_step + j, j).start()
            wdb[...] = wd_ref[0].astype(jnp.bfloat16)
            bd = bd_ref[0]

            @pl.when(col >= 2)
            def _():
                for_active(lambda s: y_copy(s, col, par).wait())

            def fn(r0, nr):
                ch = min(nr, 2 * sb)
                for c in range(nr // ch):
                    rs = slice(r0 + c * ch, r0 + (c + 1) * ch)
                    y = bd
                    for ff in range(nf):
                        y = y + jnp.dot(hdn_ref[ff, rs, :], wdb[ff * fch:(ff + 1) * fch, :],
                                        preferred_element_type=jnp.float32)
                    yst[par, rs, :] = y
            for_active_blocks(fn)

            for_active(lambda s: y_copy(s, col, par).start())

            for j in range(per_step):
                @pl.when(col * per_step + j < n_next)
                def _():
                    s_next = col * per_step + j
                    next_copy(s_next, j).wait()
                    xbf[pl.ds(pl.multiple_of(s_next * sb, sb), sb), :] = xst[j].astype(jnp.bfloat16)

            @pl.when(col == ncol - 1)
            def _drain():
                def both(s):
                    y_copy(s, col, 1 - par).wait()
                    y_copy(s, col, par).wait()
                for_active(both)


def _moe_experts(g_expert, g_sb0, g_nsb, xr, w_gate, b_gate, w_up, b_up, w_down, b_down):
    n_rows, d = xr.shape
    ne, _, dff = w_gate.shape
    fch = _tile(dff, MOE_F)
    nf = dff // fch
    fcn = _tile(d, MOE_FN)
    ncol = d // fcn
    assert ncol >= 2
    assert MOE_NSUB <= 2 * ncol
    n_groups = g_expert.shape[0]
    rg = MOE_NSUB * MOE_SB

    def gu_idx(g, t, ge, gsb, gn):
        return jnp.where(gn[g] > 0, jnp.minimum(t, nf - 1), nf - 1)

    def dn_idx(g, t, ge, gsb, gn):
        return jnp.where(gn[g] > 0, jnp.maximum(t - nf, 0), ncol - 1)

    grid_spec = pltpu.PrefetchScalarGridSpec(
        num_scalar_prefetch=3,
        grid=(n_groups, nf + ncol),
        in_specs=[
            pl.BlockSpec(memory_space=pl.ANY),
            pl.BlockSpec((1, d, fch), lambda g, t, ge, gsb, gn: (ge[g], 0, gu_idx(g, t, ge, gsb, gn))),
            pl.BlockSpec((1, 1, fch), lambda g, t, ge, gsb, gn: (ge[g], 0, gu_idx(g, t, ge, gsb, gn))),
            pl.BlockSpec((1, d, fch), lambda g, t, ge, gsb, gn: (ge[g], 0, gu_idx(g, t, ge, gsb, gn))),
            pl.BlockSpec((1, 1, fch), lambda g, t, ge, gsb, gn: (ge[g], 0, gu_idx(g, t, ge, gsb, gn))),
            pl.BlockSpec((1, dff, fcn), lambda g, t, ge, gsb, gn: (ge[g], 0, dn_idx(g, t, ge, gsb, gn))),
            pl.BlockSpec((1, 1, fcn), lambda g, t, ge, gsb, gn: (ge[g], 0, dn_idx(g, t, ge, gsb, gn))),
        ],
        out_specs=pl.BlockSpec(memory_space=pl.ANY),
        scratch_shapes=[
            pltpu.VMEM((2, MOE_SB, d), jnp.float32),
            pltpu.VMEM((rg, d), jnp.bfloat16),
            pltpu.VMEM((nf, rg, fch), jnp.bfloat16),
            pltpu.VMEM((2, rg, fcn), jnp.float32),
            pltpu.VMEM((d, fch), jnp.bfloat16),
            pltpu.VMEM((d, fch), jnp.bfloat16),
            pltpu.VMEM((dff, fcn), jnp.bfloat16),
            pltpu.SemaphoreType.DMA((2,)),
            pltpu.SemaphoreType.DMA((2,)),
        ],
    )
    return pl.pallas_call(
        functools.partial(_moe_kernel, sb=MOE_SB, nsub=MOE_NSUB, nf=nf, ncol=ncol),
        grid_spec=grid_spec,
        out_shape=jax.ShapeDtypeStruct((n_rows, d), jnp.float32),
        compiler_params=_cparams(("arbitrary", "arbitrary")),
        name="moe_experts",
    )(g_expert, g_sb0, g_nsb, xr, w_gate, b_gate.reshape(ne, 1, dff), w_up, b_up.reshape(ne, 1, dff),
      w_down, b_down.reshape(ne, 1, d))


def _combine_kernel(dest_ref, h_ref, gate_ref, g_ref, b_ref, y_hbm, o_ref, gbuf, sem, *, td, alpha):
    i = pl.program_id(0)
    n = pl.num_programs(0)
    slot = i % 2

    def row_copy(step, t, k, sl):
        d = dest_ref[(step * td + t) * TOP_K + k]
        return pltpu.make_async_copy(y_hbm.at[pl.ds(d, 1)], gbuf.at[sl, k, pl.ds(t, 1)], sem.at[sl])

    def issue(step, sl):
        def body(t, carry):
            for k in range(TOP_K):
                row_copy(step, t, k, sl).start()
            return carry
        lax.fori_loop(0, td, body, 0)

    @pl.when(i == 0)
    def _():
        issue(0, 0)

    @pl.when(i + 1 < n)
    def _():
        issue(i + 1, 1 - slot)

    def wait(t, carry):
        for k in range(TOP_K):
            pltpu.make_async_copy(y_hbm.at[pl.ds(0, 1)], gbuf.at[slot, k, pl.ds(0, 1)], sem.at[slot]).wait()
        return carry
    lax.fori_loop(0, td, wait, 0)

    gates = gate_ref[...]
    moe = gates[:, 0:1] * gbuf[slot, 0]
    for k in range(1, TOP_K):
        moe = moe + gates[:, k:k + 1] * gbuf[slot, k]
    o_ref[...] = _layer_norm(alpha * h_ref[...] + moe, g_ref[...], b_ref[...])


def _combine(dest_flat, h, gates, ln_g, ln_b, yr, alpha):
    rows, d = h.shape
    td = _tile(rows, DISPATCH_TD)
    grid_spec = pltpu.PrefetchScalarGridSpec(
        num_scalar_prefetch=1,
        grid=(rows // td,),
        in_specs=[
            pl.BlockSpec((td, d), lambda i, dest: (i, 0)),
            pl.BlockSpec((td, TOP_K), lambda i, dest: (i, 0)),
            pl.BlockSpec((1, d), lambda i, dest: (0, 0)),
            pl.BlockSpec((1, d), lambda i, dest: (0, 0)),
            pl.BlockSpec(memory_space=pl.ANY),
        ],
        out_specs=pl.BlockSpec((td, d), lambda i, dest: (i, 0)),
        scratch_shapes=[pltpu.VMEM((2, TOP_K, td, d), jnp.float32), pltpu.SemaphoreType.DMA((2,))],
    )
    return pl.pallas_call(
        functools.partial(_combine_kernel, td=td, alpha=alpha),
        grid_spec=grid_spec,
        out_shape=jax.ShapeDtypeStruct((rows, d), jnp.float32),
        compiler_params=_cparams(("arbitrary",)),
        name="combine",
    )(dest_flat, h, gates, ln_g, ln_b, yr)


def _rope_tables(first, count):
    half = QK_DIM // 2
    log_theta = jnp.log(jnp.asarray(ROPE_THETA, jnp.float32))
    inv = jnp.exp(-log_theta * jnp.arange(half, dtype=jnp.float32) * (2.0 / QK_DIM))
    pos = first + jnp.arange(count)
    ang = pos.astype(jnp.float32)[:, None] * inv[None, :]
    cos, sin = jnp.cos(ang), jnp.sin(ang)
    reps = LANES // QK_DIM
    return jnp.tile(jnp.concatenate([cos, cos], axis=-1), (1, reps)), \
        jnp.tile(jnp.concatenate([-sin, sin], axis=-1), (1, reps))


def _shifted_taps(u, state):
    b, s, c = u.shape
    full = jnp.concatenate([state, u], axis=1)
    taps = [full[:, j:j + s].reshape(b * s, c) for j in range(CONV_K)]
    return taps, full[:, s:]


def _routing_tables(counts, n_groups):
    nsb = (counts + MOE_SB - 1) // MOE_SB
    sb_end = jnp.cumsum(nsb)
    sb_off = sb_end - nsb
    ng = (nsb + MOE_NSUB - 1) // MOE_NSUB
    g_end = jnp.cumsum(ng)
    g_off = g_end - ng
    gi = jnp.arange(n_groups, dtype=jnp.int32)
    last = jnp.maximum(g_end[-1] - 1, 0)
    ge = jnp.sum((g_end[None, :] <= jnp.minimum(gi, last)[:, None]).astype(jnp.int32), axis=1)
    ge = jnp.minimum(ge, counts.shape[0] - 1)
    local = gi - g_off[ge]
    g_sb0 = sb_off[ge] + local * MOE_NSUB
    g_nsb = jnp.where(gi < g_end[-1], jnp.clip(nsb[ge] - local * MOE_NSUB, 0, MOE_NSUB), 0)
    return sb_off * MOE_SB, ge, g_sb0.astype(jnp.int32), g_nsb.astype(jnp.int32)


def kernel(x_prompt, x_sample, cache_k, cache_v, state_conv, page_table, w_in, conv_w, lambda_q, lambda_k, subln_g, w_out, ln1_g, ln1_b, w_router, b_router, w_gate, b_gate, w_up, b_up, w_down, b_down, ln2_g, ln2_b):
    bp, sp, d = x_prompt.shape
    bs, ts, _ = x_sample.shape
    depth, n_pool = cache_k.shape[0], cache_k.shape[1]
    nh = cache_k.shape[3]
    att_dim = nh * V_DIM
    conv_dim = conv_w.shape[-1]
    ne = w_router.shape[-1]
    n_past = page_table.shape[1] * PAGE_SIZE
    alpha = (2.0 * depth) ** 0.25
    assert ts == 4 and cache_k.shape[2] == PAGE_SIZE and w_in.shape[-1] == 3 * att_dim + 3 * conv_dim

    cos_p, sin_p = _rope_tables(0, sp)
    cos_s, sin_s = _rope_tables(n_past, ts)
    cos_s, sin_s = jnp.tile(cos_s, (bs, 1)), jnp.tile(sin_s, (bs, 1))
    cache_k2 = cache_k.reshape(depth * n_pool, PAGE_SIZE * nh, V_DIM)
    cache_v2 = cache_v.reshape(depth * n_pool, PAGE_SIZE * nh, V_DIM)

    n_tok = bp * sp + bs * ts
    n_rows = (n_tok * TOP_K // MOE_SB + ne) * MOE_SB
    n_groups = ne + max(n_rows // MOE_SB - ne, 0) // MOE_NSUB + 1

    xp = x_prompt.reshape(bp * sp, d)
    xs = x_sample.reshape(bs * ts, d)
    outs = {k: [] for k in ('kp', 'vp', 'cp', 'ks', 'vs', 'cs')}
    for l in range(depth):
        lam_init = 0.8 - 0.6 * math.exp(-0.3 * l)
        w_in_bf = w_in[l].astype(jnp.bfloat16)
        w_out_bf = w_out[l].astype(jnp.bfloat16)
        lq, lk, sg = lambda_q[l], lambda_k[l], subln_g[l].reshape(1, V_DIM)
        g1, b1 = ln1_g[l].reshape(1, d), ln1_b[l].reshape(1, d)
        g2, b2 = ln2_g[l].reshape(1, d), ln2_b[l].reshape(1, d)
        br = b_router[l].reshape(1, ne)

        xp_bf = xp.astype(jnp.bfloat16)
        q_bf, k_f, k_bf, v_f, _, vt_bf = _qkv_proj(xp_bf, w_in_bf, cos_p, sin_p, att_dim)
        u_p, gb_p = _conv_proj(xp_bf, w_in_bf, att_dim, conv_dim)
        att_p = _flash_prompt(q_bf, k_bf, vt_bf, lq, lk, sg.reshape(V_DIM, 1), bp, sp, lam_init)
        taps, cst_p = _shifted_taps(u_p.reshape(bp, sp, conv_dim), jnp.zeros((bp, CONV_K - 1, conv_dim), u_p.dtype))
        conv_p = _conv_gate(*taps, gb_p, conv_w[l])
        outs['kp'].append(k_f.reshape(bp, sp, nh, V_DIM))
        outs['vp'].append(v_f.reshape(bp, sp, nh, V_DIM))
        outs['cp'].append(cst_p)

        xs_bf = xs.astype(jnp.bfloat16)
        qs_bf, ks_f, ks_bf, vs_f, vs_bf, _ = _qkv_proj(xs_bf, w_in_bf, cos_s, sin_s, att_dim)
        u_s, gb_s = _conv_proj(xs_bf, w_in_bf, att_dim, conv_dim)
        q5 = qs_bf.reshape(bs, ts, nh, 2, QK_DIM)
        eye = jnp.eye(2, dtype=q5.dtype)
        qall = jnp.einsum('bthmd,mn->bhmtnd', q5, eye).reshape(bs, nh * 2 * ts, V_DIM)
        att_s = _sample_attention(qall, ks_bf.reshape(bs, ts * nh, V_DIM), vs_bf.reshape(bs, ts * nh, V_DIM),
                                  cache_k2, cache_v2, page_table, l, n_pool, lq, lk, sg, ts, lam_init)
        att_s = att_s.reshape(bs, nh, 2 * ts, V_DIM)[:, :, :ts]
        att_s = jnp.transpose(att_s, (0, 2, 1, 3)).reshape(bs * ts, att_dim).astype(jnp.bfloat16)
        taps, cst_s = _shifted_taps(u_s.reshape(bs, ts, conv_dim), state_conv[l])
        conv_s = _conv_gate(*taps, gb_s, conv_w[l])
        outs['ks'].append(ks_f.reshape(bs, ts, nh, V_DIM))
        outs['vs'].append(vs_f.reshape(bs, ts, nh, V_DIM))
        outs['cs'].append(cst_s)

        cnt0 = jnp.zeros((1, ne), jnp.float32)
        h_p, ids_p, rank_p, gate_p, cnt_p = _out_proj_route(att_p, conv_p, xp, w_out_bf, g1, b1, w_router[l], br, cnt0, alpha)
        h_s, ids_s, rank_s, gate_s, cnt_all = _out_proj_route(att_s, conv_s, xs, w_out_bf, g1, b1, w_router[l], br, cnt_p, alpha)

        counts = cnt_all.reshape(ne).astype(jnp.int32)
        row_off, g_expert, g_sb0, g_nsb = _routing_tables(counts, n_groups)
        dest_p = (row_off[ids_p.reshape(-1)] + rank_p.reshape(-1)).astype(jnp.int32)
        dest_s = (row_off[ids_s.reshape(-1)] + rank_s.reshape(-1)).astype(jnp.int32)

        xr = _dispatch(dest_p, h_p, None, n_rows)
        xr = _dispatch(dest_s, h_s, xr, n_rows)
        yr = _moe_experts(g_expert, g_sb0, g_nsb, xr, w_gate[l], b_gate[l], w_up[l], b_up[l], w_down[l], b_down[l])
        xp = _combine(dest_p, h_p, gate_p, g2, b2, yr, alpha)
        xs = _combine(dest_s, h_s, gate_s, g2, b2, yr, alpha)

    return (xp.reshape(bp, sp, d), xs.reshape(bs, ts, d),
            jnp.stack(outs['kp']), jnp.stack(outs['vp']), jnp.stack(outs['cp']),
            jnp.stack(outs['ks']), jnp.stack(outs['vs']), jnp.stack(outs['cs']))
```

```python
import functools
import math

import jax
import jax.numpy as jnp
from jax import lax
from jax.experimental import pallas as pl
from jax.experimental.pallas import tpu as pltpu

QK_DIM = 64
V_DIM = 2 * QK_DIM
CONV_K = 3
PAGE_SIZE = 128
ROPE_THETA = 10000.0
SUBLN_EPS = 1e-5
LN_EPS = 1e-5
TOP_K = 4
SWIGLU_ALPHA = 1.702
SWIGLU_LIMIT = 7.0

LANES = 128
NEG_BIG = -1e30
Q_SCALE = QK_DIM ** -0.5 * math.log2(math.e)

FLASH_T = 512
FLASH_HEADS = 8
FLASH_CW = 256

VMEM_LIMIT = 56 * 1024 * 1024

MOE_SB = 256
MOE_NSUB = 8
MOE_F = 256
MOE_FN = 512
DISPATCH_TD = 128
SAMPLE_PAGES = 16
SAMPLE_SUB_PAGES = 8


def _cparams(sem):
    return pltpu.CompilerParams(dimension_semantics=sem, vmem_limit_bytes=VMEM_LIMIT)


def _tile(n, pref):
    t = min(n, pref)
    while n % t:
        t //= 2
    return t


def _rope_chunks(a, cos, sin_signed):
    lane = lax.broadcasted_iota(jnp.int32, (a.shape[0], LANES), 1)
    first = (lane % QK_DIM) < (QK_DIM // 2)
    outs = []
    for c in range(a.shape[1] // LANES):
        ac = a[:, c * LANES:(c + 1) * LANES]
        partner = jnp.where(first, pltpu.roll(ac, LANES - QK_DIM // 2, 1), pltpu.roll(ac, QK_DIM // 2, 1))
        outs.append(ac * cos + partner * sin_signed)
    return outs


def _qkv_kernel(x_ref, w_ref, cos_ref, sin_ref, qb_ref, kf_ref, kb_ref, vf_ref, vb_ref, vt_ref, *, nsec, rc):
    j = pl.program_id(1)
    tm = x_ref.shape[0]

    def chunks(epilogue):
        for r in range(tm // rc):
            rs = slice(r * rc, (r + 1) * rc)
            epilogue(rs, jnp.dot(x_ref[rs, :], w_ref[...], preferred_element_type=jnp.float32))

    @pl.when(j < nsec)
    def _q():
        def epilogue(rs, acc):
            for c, o in enumerate(_rope_chunks(acc, cos_ref[rs, :], sin_ref[rs, :])):
                qb_ref[rs, c * LANES:(c + 1) * LANES] = (o * Q_SCALE).astype(qb_ref.dtype)
        chunks(epilogue)

    @pl.when((j >= nsec) & (j < 2 * nsec))
    def _k():
        def epilogue(rs, acc):
            for c, o in enumerate(_rope_chunks(acc, cos_ref[rs, :], sin_ref[rs, :])):
                kf_ref[rs, c * LANES:(c + 1) * LANES] = o
                kb_ref[rs, c * LANES:(c + 1) * LANES] = o.astype(kb_ref.dtype)
        chunks(epilogue)

    @pl.when(j >= 2 * nsec)
    def _v():
        def epilogue(rs, acc):
            vf_ref[rs, :] = acc
            vb_ref[rs, :] = acc.astype(vb_ref.dtype)
            vt_ref[:, rs] = acc.T.astype(vt_ref.dtype)
        chunks(epilogue)


def _qkv_proj(x_bf, w_bf, cos, sin_signed, att_dim):
    rows, d = x_bf.shape
    tm = _tile(rows, 1024)
    tn = _tile(att_dim, 512)
    nsec = att_dim // tn
    table_blocks = cos.shape[0] // tm
    assert cos.shape[0] % tm == 0
    sec = lambda s: (lambda i, j: (i, jnp.clip(j - s * nsec, 0, nsec - 1)))
    f32 = jax.ShapeDtypeStruct((rows, att_dim), jnp.float32)
    b16 = jax.ShapeDtypeStruct((rows, att_dim), jnp.bfloat16)
    return pl.pallas_call(
        functools.partial(_qkv_kernel, nsec=nsec, rc=_tile(tm, 256)),
        grid=(rows // tm, 3 * nsec),
        in_specs=[
            pl.BlockSpec((tm, d), lambda i, j: (i, 0)),
            pl.BlockSpec((d, tn), lambda i, j: (0, j)),
            pl.BlockSpec((tm, LANES), lambda i, j: (i % table_blocks, 0)),
            pl.BlockSpec((tm, LANES), lambda i, j: (i % table_blocks, 0)),
        ],
        out_specs=[
            pl.BlockSpec((tm, tn), sec(0)),
            pl.BlockSpec((tm, tn), sec(1)),
            pl.BlockSpec((tm, tn), sec(1)),
            pl.BlockSpec((tm, tn), sec(2)),
            pl.BlockSpec((tm, tn), sec(2)),
            pl.BlockSpec((tn, tm), lambda i, j: (jnp.clip(j - 2 * nsec, 0, nsec - 1), i)),
        ],
        out_shape=[b16, f32, b16, f32, b16, jax.ShapeDtypeStruct((att_dim, rows), jnp.bfloat16)],
        compiler_params=_cparams(("parallel", "arbitrary")),
        name="qkv_proj",
    )(x_bf, w_bf, cos, sin_signed)


def _convproj_kernel(x_ref, wh_ref, wb_ref, wc_ref, u_ref, gb_ref):
    x = x_ref[...]
    h = jnp.dot(x, wh_ref[...], preferred_element_type=jnp.float32)
    gc = jnp.dot(x, wc_ref[...], preferred_element_type=jnp.float32)
    u_ref[...] = gc * h
    gb_ref[...] = jnp.dot(x, wb_ref[...], preferred_element_type=jnp.float32)


def _conv_proj(x_bf, w_bf, att_dim, conv_dim):
    rows, d = x_bf.shape
    tm = _tile(rows, 1024)
    tn = _tile(conv_dim, 512)
    base = 3 * att_dim // tn
    nsec = conv_dim // tn
    wspec = lambda s: pl.BlockSpec((d, tn), lambda i, j: (0, base + s * nsec + j))
    f32 = jax.ShapeDtypeStruct((rows, conv_dim), jnp.float32)
    return pl.pallas_call(
        _convproj_kernel,
        grid=(rows // tm, nsec),
        in_specs=[pl.BlockSpec((tm, d), lambda i, j: (i, 0)), wspec(0), wspec(1), wspec(2)],
        out_specs=[pl.BlockSpec((tm, tn), lambda i, j: (i, j))] * 2,
        out_shape=[f32, f32],
        compiler_params=_cparams(("parallel", "arbitrary")),
        name="conv_proj",
    )(x_bf, w_bf, w_bf, w_bf)


def _conv_gate_kernel(s0_ref, s1_ref, s2_ref, gb_ref, w_ref, o_ref):
    w = w_ref[...]
    y = s0_ref[...] * w[0:1] + s1_ref[...] * w[1:2] + s2_ref[...] * w[2:3]
    o_ref[...] = (gb_ref[...] * y).astype(o_ref.dtype)


def _conv_gate(s0, s1, s2, gb, conv_w):
    rows, c = gb.shape
    tm = _tile(rows, 512)
    blk = pl.BlockSpec((tm, c), lambda i: (i, 0))
    return pl.pallas_call(
        _conv_gate_kernel,
        grid=(rows // tm,),
        in_specs=[blk, blk, blk, blk, pl.BlockSpec((CONV_K, c), lambda i: (0, 0))],
        out_specs=blk,
        out_shape=jax.ShapeDtypeStruct((rows, c), jnp.bfloat16),
        compiler_params=_cparams(("parallel",)),
        name="conv_gate",
    )(s0, s1, s2, gb, conv_w)


def _conv_gate_halo_kernel(u_ref, halo_ref, gb_ref, w_ref, o_ref, *, blocks_per_seq):
    i = pl.program_id(0)
    tm = u_ref.shape[0]
    u = u_ref[...]
    halo = jnp.where(i % blocks_per_seq == 0, 0.0, halo_ref[...])
    full = jnp.concatenate([halo, u], axis=0)
    w = w_ref[...]
    y = full[6:6 + tm] * w[0:1] + full[7:7 + tm] * w[1:2] + u * w[2:3]
    o_ref[...] = (gb_ref[...] * y).astype(o_ref.dtype)


def _conv_gate_halo(u, gb, conv_w, seq):
    rows, c = gb.shape
    tm = _tile(seq, 512)
    blk = pl.BlockSpec((tm, c), lambda i: (i, 0))
    halo = pl.BlockSpec((8, c), lambda i: (jnp.maximum(i * (tm // 8) - 1, 0), 0))
    return pl.pallas_call(
        functools.partial(_conv_gate_halo_kernel, blocks_per_seq=seq // tm),
        grid=(rows // tm,),
        in_specs=[blk, halo, blk, pl.BlockSpec((CONV_K, c), lambda i: (0, 0))],
        out_specs=blk,
        out_shape=jax.ShapeDtypeStruct((rows, c), jnp.bfloat16),
        compiler_params=_cparams(("parallel",)),
        name="conv_gate_halo",
    )(u, u, gb, conv_w)


def _lambda_full(lq_ref, lk_ref, lam_init):
    prod = lq_ref[...] * lk_ref[...]
    e = jnp.exp(jnp.sum(prod, axis=-1, keepdims=True))
    return e[0:1] - e[1:2] + lam_init


def _subln(o, g, lam_init):
    ms = jnp.mean(o * o, axis=-1, keepdims=True)
    return o * lax.rsqrt(ms + SUBLN_EPS) * g * (1.0 - lam_init)


def _flash_kernel(iq_ref, ik_ref, q_ref, k_ref, vt_ref, lq_ref, lk_ref, g_ref, o_ref, qq_ref, m_ref, l_ref, acc_ref,
                  *, tq, hp, cw, lam_init):
    pair = pl.program_id(2)
    iq = iq_ref[pair]
    ik = ik_ref[pair]
    nc = 2 * tq // cw

    @pl.when(ik == 0)
    def _init():
        for h in range(hp):
            q = q_ref[:, h * V_DIM:(h + 1) * V_DIM]
            lane = lax.broadcasted_iota(jnp.int32, q.shape, 1)
            zero = jnp.zeros_like(q)
            qq_ref[h, 0:tq] = jnp.where(lane < QK_DIM, q, zero)
            qq_ref[h, tq:2 * tq] = jnp.where(lane >= QK_DIM, q, zero)
        m_ref[...] = jnp.full(m_ref.shape, NEG_BIG, jnp.float32)
        l_ref[...] = jnp.zeros(l_ref.shape, jnp.float32)
        acc_ref[...] = jnp.zeros(acc_ref.shape, jnp.float32)

    def step(masked):
        scores = []
        for h in range(hp):
            kk = k_ref[:, h * V_DIM:(h + 1) * V_DIM]
            for c in range(nc):
                s = lax.dot_general(kk, qq_ref[h, c * cw:(c + 1) * cw, :], (((1,), (1,)), ((), ())),
                                    preferred_element_type=jnp.float32)
                if masked:
                    key = lax.broadcasted_iota(jnp.int32, s.shape, 0)
                    qry = lax.broadcasted_iota(jnp.int32, s.shape, 1) + ((c * cw) % tq)
                    s = jnp.where(key <= qry, s, NEG_BIG)
                scores.append(s)
        for h in range(hp):
            vt = vt_ref[h * V_DIM:(h + 1) * V_DIM, :]
            m_all = m_ref[h]
            l_all = l_ref[h]
            m_out, l_out = [], []
            for c in range(nc):
                cs = slice(c * cw, (c + 1) * cw)
                s = scores[h * nc + c]
                m_prev = m_all[:, cs]
                m_new = jnp.maximum(m_prev, jnp.max(s, axis=0, keepdims=True))
                alpha = jnp.exp2(m_prev - m_new)
                p = jnp.exp2(s - m_new)
                l_out.append(alpha * l_all[:, cs] + jnp.sum(p, axis=0, keepdims=True))
                m_out.append(m_new)
                acc_ref[h, :, cs] = alpha * acc_ref[h, :, cs] + jnp.dot(vt, p.astype(vt.dtype),
                                                                        preferred_element_type=jnp.float32)
            m_ref[h] = jnp.concatenate(m_out, axis=1)
            l_ref[h] = jnp.concatenate(l_out, axis=1)

    @pl.when(ik < iq)
    def _off_diag():
        step(False)

    @pl.when(ik == iq)
    def _diag():
        step(True)
        lam = _lambda_full(lq_ref, lk_ref, lam_init)
        for h in range(hp):
            o = acc_ref[h] / l_ref[h]
            o = o[:, 0:tq] - lam * o[:, tq:2 * tq]
            ms = jnp.mean(o * o, axis=0, keepdims=True)
            o = o * lax.rsqrt(ms + SUBLN_EPS) * g_ref[...] * (1.0 - lam_init)
            o_ref[:, h * V_DIM:(h + 1) * V_DIM] = o.T.astype(o_ref.dtype)


def _flash_prompt(q_bf, k_bf, vt_bf, lam_q, lam_k, subln_g_col, batch, seq, lam_init):
    rows, att_dim = q_bf.shape
    nh = att_dim // V_DIM
    hp = _tile(nh, FLASH_HEADS)
    tq = _tile(seq, FLASH_T)
    cw = _tile(2 * tq, FLASH_CW)
    nq = seq // tq
    pairs = [(i, k) for i in range(nq) for k in range(i + 1)]
    iq_tab = jnp.asarray([p[0] for p in pairs], jnp.int32)
    ik_tab = jnp.asarray([p[1] for p in pairs], jnp.int32)
    qspec = pl.BlockSpec((tq, hp * V_DIM), lambda b, h, p, iqt, ikt: (b * nq + iqt[p], h))
    kspec = pl.BlockSpec((tq, hp * V_DIM), lambda b, h, p, iqt, ikt: (b * nq + ikt[p], h))
    vspec = pl.BlockSpec((hp * V_DIM, tq), lambda b, h, p, iqt, ikt: (h, b * nq + ikt[p]))
    small = lambda shape: pl.BlockSpec(shape, lambda b, h, p, iqt, ikt: (0, 0))
    grid_spec = pltpu.PrefetchScalarGridSpec(
        num_scalar_prefetch=2,
        grid=(batch, nh // hp, len(pairs)),
        in_specs=[qspec, kspec, vspec, small((2, QK_DIM)), small((2, QK_DIM)), small((V_DIM, 1))],
        out_specs=qspec,
        scratch_shapes=[
            pltpu.VMEM((hp, 2 * tq, V_DIM), jnp.bfloat16),
            pltpu.VMEM((hp, 1, 2 * tq), jnp.float32),
            pltpu.VMEM((hp, 1, 2 * tq), jnp.float32),
            pltpu.VMEM((hp, V_DIM, 2 * tq), jnp.float32),
        ],
    )
    return pl.pallas_call(
        functools.partial(_flash_kernel, tq=tq, hp=hp, cw=cw, lam_init=lam_init),
        grid_spec=grid_spec,
        out_shape=jax.ShapeDtypeStruct((rows, att_dim), jnp.bfloat16),
        compiler_params=_cparams(("parallel", "parallel", "arbitrary")),
        name="flash_prompt",
    )(iq_tab, ik_tab, q_bf, k_bf, vt_bf, lam_q, lam_k, subln_g_col)


def _sample_attn_kernel(pt_ref, qall_ref, knew_ref, vnew_ref, bias_ref, bias_new_ref, lq_ref, lk_ref, g_ref,
                        ck_hbm, cv_hbm, o_ref, kbuf, vbuf, ksem, vsem, m_ref, l_ref, acc_ref,
                        *, n_chunks, pages, page_rows, pool_base, lam_init):
    b = pl.program_id(0)
    c = pl.program_id(1)
    step = b * n_chunks + c
    n_steps = pl.num_programs(0) * n_chunks
    slot = step % 2

    def copies(bb, cc, sl):
        out = []
        for j in range(pages):
            page = pt_ref[bb, cc * pages + j] + pool_base
            dst = pl.ds(j * page_rows, page_rows)
            out.append(pltpu.make_async_copy(ck_hbm.at[page], kbuf.at[sl, dst], ksem.at[sl]))
            out.append(pltpu.make_async_copy(cv_hbm.at[page], vbuf.at[sl, dst], vsem.at[sl]))
        return out

    @pl.when(step == 0)
    def _prime():
        for cp in copies(b, c, slot):
            cp.start()

    @pl.when(step + 1 < n_steps)
    def _prefetch():
        nxt = step + 1
        for cp in copies(nxt // n_chunks, nxt % n_chunks, 1 - slot):
            cp.start()

    @pl.when(c == 0)
    def _init():
        m_ref[...] = jnp.full(m_ref.shape, NEG_BIG, jnp.float32)
        l_ref[...] = jnp.zeros(l_ref.shape, jnp.float32)
        acc_ref[...] = jnp.zeros(acc_ref.shape, jnp.float32)

    for cp in copies(b, c, slot):
        cp.wait()

    qall = qall_ref[0]

    def update(kk, vv, bias):
        s = lax.dot_general(qall, kk, (((1,), (1,)), ((), ())), preferred_element_type=jnp.float32) + bias
        m_prev = m_ref[...]
        m_new = jnp.maximum(m_prev, jnp.max(s, axis=-1, keepdims=True))
        alpha = jnp.exp2(m_prev - m_new)
        p = jnp.exp2(s - m_new)
        l_ref[...] = alpha * l_ref[...] + jnp.sum(p, axis=-1, keepdims=True)
        acc_ref[...] = alpha * acc_ref[...] + jnp.dot(p.astype(vv.dtype), vv, preferred_element_type=jnp.float32)
        m_ref[...] = m_new

    sub = bias_ref.shape[1]
    for u in range(pages * page_rows // sub):
        rows = pl.ds(u * sub, sub)
        update(kbuf[slot, rows, :].astype(jnp.bfloat16), vbuf[slot, rows, :].astype(jnp.bfloat16), bias_ref[...])

    @pl.when(c == n_chunks - 1)
    def _finish():
        update(knew_ref[0], vnew_ref[0], bias_new_ref[...])
        o = acc_ref[...] / l_ref[...]
        lam = _lambda_full(lq_ref, lk_ref, lam_init)
        g = g_ref[...]
        for h in range(o.shape[0] // 8):
            oh = o[h * 8:(h + 1) * 8]
            oh = oh - lam * pltpu.roll(oh, 4, 0)
            o_ref[0, h * 8:(h + 1) * 8, :] = _subln(oh, g, lam_init)


def _sample_attention(qall, knew, vnew, cache_k2, cache_v2, page_table, layer, n_pool, lam_q, lam_k, subln_g,
                      dec_seq, lam_init):
    nb, nrow, _ = qall.shape
    nh = nrow // 8
    n_pages = page_table.shape[1]
    pages = _tile(n_pages, SAMPLE_PAGES)
    n_chunks = n_pages // pages
    page_rows = PAGE_SIZE * nh
    ncols = pages * page_rows
    sub_cols = _tile(pages, SAMPLE_SUB_PAGES) * page_rows
    row = jnp.arange(nrow)
    col = jnp.arange(sub_cols)
    bias = jnp.where((col[None, :] % nh) == (row[:, None] // 8), 0.0, NEG_BIG).astype(jnp.float32)
    coln = jnp.arange(dec_seq * nh)
    ok = ((coln[None, :] % nh) == (row[:, None] // 8)) & ((coln[None, :] // nh) <= (row[:, None] % 4))
    bias_new = jnp.where(ok, 0.0, NEG_BIG).astype(jnp.float32)
    const = lambda shape: pl.BlockSpec(shape, lambda b, c, pt: (0,) * len(shape))
    per_seq = lambda shape: pl.BlockSpec(shape, lambda b, c, pt: (b,) + (0,) * (len(shape) - 1))
    grid_spec = pltpu.PrefetchScalarGridSpec(
        num_scalar_prefetch=1,
        grid=(nb, n_chunks),
        in_specs=[
            per_seq((1, nrow, V_DIM)),
            per_seq((1, dec_seq * nh, V_DIM)),
            per_seq((1, dec_seq * nh, V_DIM)),
            const((nrow, sub_cols)),
            const((nrow, dec_seq * nh)),
            const((2, QK_DIM)),
            const((2, QK_DIM)),
            const((1, V_DIM)),
            pl.BlockSpec(memory_space=pl.ANY),
            pl.BlockSpec(memory_space=pl.ANY),
        ],
        out_specs=per_seq((1, nrow, V_DIM)),
        scratch_shapes=[
            pltpu.VMEM((2, ncols, V_DIM), jnp.float32),
            pltpu.VMEM((2, ncols, V_DIM), jnp.float32),
            pltpu.SemaphoreType.DMA((2,)),
            pltpu.SemaphoreType.DMA((2,)),
            pltpu.VMEM((nrow, 1), jnp.float32),
            pltpu.VMEM((nrow, 1), jnp.float32),
            pltpu.VMEM((nrow, V_DIM), jnp.float32),
        ],
    )
    return pl.pallas_call(
        functools.partial(_sample_attn_kernel, n_chunks=n_chunks, pages=pages, page_rows=page_rows,
                          pool_base=layer * n_pool, lam_init=lam_init),
        grid_spec=grid_spec,
        out_shape=jax.ShapeDtypeStruct((nb, nrow, V_DIM), jnp.float32),
        compiler_params=_cparams(("arbitrary", "arbitrary")),
        name="sample_attn",
    )(page_table, qall, knew, vnew, bias, bias_new, lam_q, lam_k, subln_g, cache_k2, cache_v2)


def _layer_norm(y, g, b):
    mu = jnp.mean(y, axis=-1, keepdims=True)
    yc = y - mu
    var = jnp.mean(yc * yc, axis=-1, keepdims=True)
    return yc * lax.rsqrt(var + LN_EPS) * g + b


def _outproj_kernel(att_ref, conv_ref, x_ref, wa_ref, wc_ref, g_ref, b_ref, wr_ref, br_ref, cnt0_ref,
                    h_ref, ids_ref, rank_ref, gate_ref, cnt_ref, cnt_scr, *, alpha, rc):
    i = pl.program_id(0)

    @pl.when(i == 0)
    def _():
        cnt_scr[...] = cnt0_ref[...]

    wr = wr_ref[...]
    w_hi = wr.astype(jnp.bfloat16)
    w_lo = (wr - w_hi.astype(jnp.float32)).astype(jnp.bfloat16)
    ne = wr.shape[1]
    lane = lax.broadcasted_iota(jnp.int32, (rc, ne), 1).astype(jnp.float32)
    r = lax.broadcasted_iota(jnp.int32, (rc, rc), 0)
    c = lax.broadcasted_iota(jnp.int32, (rc, rc), 1)
    tri = jnp.where(c < r, 1.0, 0.0).astype(jnp.bfloat16)
    cnt = cnt_scr[...]

    for q in range(att_ref.shape[0] // rc):
        rs = slice(q * rc, (q + 1) * rc)
        mixed = jnp.dot(att_ref[rs, :], wa_ref[...], preferred_element_type=jnp.float32)
        mixed = mixed + jnp.dot(conv_ref[rs, :], wc_ref[...], preferred_element_type=jnp.float32)
        h = _layer_norm(alpha * x_ref[rs, :] + mixed, g_ref[...], b_ref[...])
        h_ref[rs, :] = h

        h_hi = h.astype(jnp.bfloat16)
        h_lo = (h - h_hi.astype(jnp.float32)).astype(jnp.bfloat16)
        logits = (jnp.dot(h_hi, w_hi, preferred_element_type=jnp.float32)
                  + jnp.dot(h_hi, w_lo, preferred_element_type=jnp.float32)
                  + jnp.dot(h_lo, w_hi, preferred_element_type=jnp.float32)) + br_ref[...]

        work = logits
        vals, ids, hot = [], [], []
        for _ in range(TOP_K):
            m = jnp.max(work, axis=-1, keepdims=True)
            idx = jnp.min(jnp.where(work == m, lane, float(ne)), axis=-1, keepdims=True)
            sel = lane == idx
            vals.append(m)
            ids.append(idx.astype(jnp.int32))
            hot.append(sel)
            work = jnp.where(sel, -jnp.inf, work)
        exps = [jnp.exp(v - vals[0]) for v in vals]
        denom = exps[0] + exps[1] + exps[2] + exps[3]

        onehot = jnp.where(hot[0] | hot[1] | hot[2] | hot[3], 1.0, 0.0)
        before = jnp.dot(tri, onehot.astype(jnp.bfloat16), preferred_element_type=jnp.float32) + cnt
        for k in range(TOP_K):
            ids_ref[rs, k:k + 1] = ids[k]
            gate_ref[rs, k:k + 1] = exps[k] / denom
            rank = jnp.sum(jnp.where(hot[k], before, 0.0), axis=-1, keepdims=True)
            rank_ref[rs, k:k + 1] = rank.astype(jnp.int32)
        cnt = cnt + jnp.sum(onehot, axis=0, keepdims=True)

    cnt_scr[...] = cnt
    cnt_ref[...] = cnt


def _out_proj_route(att_bf, conv_bf, x, w_out_bf, ln_g, ln_b, w_router, b_router, cnt0, alpha):
    rows, d = x.shape
    ka = att_bf.shape[1]
    kc = conv_bf.shape[1]
    ne = w_router.shape[1]
    tm = _tile(rows, 512)
    rc = _tile(tm, 256)
    assert ka == kc
    row = lambda n: pl.BlockSpec((tm, n), lambda i: (i, 0))
    const = lambda shape, idx=(0, 0): pl.BlockSpec(shape, lambda i: idx)
    i32 = jax.ShapeDtypeStruct((rows, TOP_K), jnp.int32)
    return pl.pallas_call(
        functools.partial(_outproj_kernel, alpha=alpha, rc=rc),
        grid=(rows // tm,),
        in_specs=[row(ka), row(kc), row(d), const((ka, d), (0, 0)), const((kc, d), (1, 0)),
                  const((1, d)), const((1, d)), const((d, ne)), const((1, ne)), const((1, ne))],
        out_specs=[row(d), row(TOP_K), row(TOP_K), row(TOP_K), const((1, ne))],
        out_shape=[jax.ShapeDtypeStruct((rows, d), jnp.float32), i32, i32,
                   jax.ShapeDtypeStruct((rows, TOP_K), jnp.float32),
                   jax.ShapeDtypeStruct((1, ne), jnp.float32)],
        scratch_shapes=[pltpu.VMEM((1, ne), jnp.float32)],
        compiler_params=_cparams(("arbitrary",)),
        name="out_proj_route",
    )(att_bf, conv_bf, x, w_out_bf, w_out_bf, ln_g, ln_b, w_router, b_router, cnt0)


def _dispatch_kernel(dest_ref, h_ref, *rest, td):
    xr_ref, sem = rest[-2], rest[-1]
    i = pl.program_id(0)

    def row_copy(t, d):
        return pltpu.make_async_copy(h_ref.at[pl.ds(t, 1)], xr_ref.at[pl.ds(d, 1)], sem.at[0])

    def start(t, carry):
        for k in range(TOP_K):
            row_copy(t, dest_ref[(i * td + t) * TOP_K + k]).start()
        return carry

    def wait(t, carry):
        for k in range(TOP_K):
            row_copy(0, 0).wait()
        return carry

    lax.fori_loop(0, td, start, 0)
    lax.fori_loop(0, td, wait, 0)


def _dispatch(dest_flat, h, xr, n_rows):
    rows, d = h.shape
    td = _tile(rows, DISPATCH_TD)
    in_specs = [pl.BlockSpec((td, d), lambda i, dest: (i, 0))]
    args = [dest_flat, h]
    aliases = {}
    if xr is not None:
        in_specs.append(pl.BlockSpec(memory_space=pl.ANY))
        args.append(xr)
        aliases = {2: 0}
    grid_spec = pltpu.PrefetchScalarGridSpec(
        num_scalar_prefetch=1,
        grid=(rows // td,),
        in_specs=in_specs,
        out_specs=pl.BlockSpec(memory_space=pl.ANY),
        scratch_shapes=[pltpu.SemaphoreType.DMA((1,))],
    )
    return pl.pallas_call(
        functools.partial(_dispatch_kernel, td=td),
        grid_spec=grid_spec,
        out_shape=jax.ShapeDtypeStruct((n_rows, d), jnp.float32),
        input_output_aliases=aliases,
        compiler_params=_cparams(("arbitrary",)),
        name="dispatch",
    )(*args)


def _moe_kernel(ge_ref, gsb_ref, gn_ref, xr_hbm, wg_ref, bg_ref, wu_ref, bu_ref, wd_ref, bd_ref, y_hbm,
                xst, xbf, hdn_ref, yst, wgb, wub, wdb, xsem, ysem, *, sb, nsub, nf, ncol):
    g = pl.program_id(0)
    t = pl.program_id(1)
    n = gn_ref[g]
    sb0 = gsb_ref[g]
    fch = wgb.shape[1]
    fcn = wdb.shape[1]

    def x_copy(s):
        return pltpu.make_async_copy(xr_hbm.at[pl.ds((sb0 + s) * sb, sb)], xst.at[s % 2], xsem.at[s % 2])

    def y_copy(s, col, par):
        return pltpu.make_async_copy(yst.at[par, pl.ds(s * sb, sb)],
                                     y_hbm.at[pl.ds((sb0 + s) * sb, sb), pl.ds(col * fcn, fcn)], ysem.at[par])

    def for_active(fn):
        for s in range(nsub):
            @pl.when(s < n)
            def _():
                fn(s)

    def for_active_blocks(fn):
        for s in range(0, nsub, 4):
            left = n - s

            @pl.when(left >= 4)
            def _():
                fn(s * sb, 4 * sb)

            @pl.when((left == 2) | (left == 3))
            def _():
                fn(s * sb, 2 * sb)

            @pl.when(left == 3)
            def _():
                fn((s + 2) * sb, sb)

            @pl.when(left == 1)
            def _():
                fn(s * sb, sb)

    g_next = jnp.minimum(g + 1, pl.num_programs(0) - 1)
    n_next = jnp.where(g + 1 < pl.num_programs(0), gn_ref[g_next], 0)
    sb0_next = gsb_ref[g_next]
    per_step = -(-nsub // ncol)

    def next_copy(s, j):
        return pltpu.make_async_copy(xr_hbm.at[pl.ds((sb0_next + s) * sb, sb)], xst.at[j], xsem.at[j])

    @pl.when(n > 0)
    def _group():
        @pl.when((t == 0) & (g == 0))
        def _load():
            x_copy(0).start()
            for s in range(nsub):
                @pl.when(s < n)
                def _():
                    if s + 1 < nsub:
                        @pl.when(s + 1 < n)
                        def _():
                            x_copy(s + 1).start()
                    x_copy(s).wait()
                    xbf[s * sb:(s + 1) * sb, :] = xst[s % 2].astype(jnp.bfloat16)

        @pl.when(t < nf)
        def _gate_up():
            wgb[...] = wg_ref[0].astype(jnp.bfloat16)
            wub[...] = wu_ref[0].astype(jnp.bfloat16)
            bg = bg_ref[0]
            bu = bu_ref[0]

            def fn(r0, nr):
                for c in range(nr // sb):
                    rs = slice(r0 + c * sb, r0 + (c + 1) * sb)
                    x = xbf[rs, :]
                    gg = jnp.dot(x, wgb[...], preferred_element_type=jnp.float32) + bg
                    uu = jnp.dot(x, wub[...], preferred_element_type=jnp.float32) + bu
                    gg = jnp.minimum(gg, SWIGLU_LIMIT)
                    uu = jnp.clip(uu, -SWIGLU_LIMIT, SWIGLU_LIMIT)
                    hdn = gg * jax.nn.sigmoid(SWIGLU_ALPHA * gg) * (uu + 1.0)
                    hdn_ref[t, rs, :] = hdn.astype(jnp.bfloat16)
            for_active_blocks(fn)

        @pl.when(t >= nf)
        def _down():
            col = t - nf
            par = col % 2
            for j in range(per_step):
                @pl.when(col * per_step + j < n_next)
                def _():
                    next_copy(col * per_step + j, j).start()
            wdb[...] = wd_ref[0].astype(jnp.bfloat16)
            bd = bd_ref[0]

            @pl.when(col >= 2)
            def _():
                for_active(lambda s: y_copy(s, col, par).wait())

            def fn(r0, nr):
                ch = min(nr, 2 * sb)
                for c in range(nr // ch):
                    rs = slice(r0 + c * ch, r0 + (c + 1) * ch)
                    y = bd
                    for ff in range(nf):
                        y = y + jnp.dot(hdn_ref[ff, rs, :], wdb[ff * fch:(ff + 1) * fch, :],
                                        preferred_element_type=jnp.float32)
                    yst[par, rs, :] = y
            for_active_blocks(fn)

            for_active(lambda s: y_copy(s, col, par).start())

            for j in range(per_step):
                @pl.when(col * per_step + j < n_next)
                def _():
                    s_next = col * per_step + j
                    next_copy(s_next, j).wait()
                    xbf[pl.ds(pl.multiple_of(s_next * sb, sb), sb), :] = xst[j].astype(jnp.bfloat16)

            @pl.when(col == ncol - 1)
            def _drain():
                def both(s):
                    y_copy(s, col, 1 - par).wait()
                    y_copy(s, col, par).wait()
                for_active(both)


def _moe_experts(g_expert, g_sb0, g_nsb, xr, w_gate, b_gate, w_up, b_up, w_down, b_down):
    n_rows, d = xr.shape
    ne, _, dff = w_gate.shape
    fch = _tile(dff, MOE_F)
    nf = dff // fch
    fcn = _tile(d, MOE_FN)
    ncol = d // fcn
    assert ncol >= 2
    assert MOE_NSUB <= 2 * ncol
    n_groups = g_expert.shape[0]
    rg = MOE_NSUB * MOE_SB

    def gu_idx(g, t, ge, gsb, gn):
        return jnp.where(gn[g] > 0, jnp.minimum(t, nf - 1), nf - 1)

    def dn_idx(g, t, ge, gsb, gn):
        return jnp.where(gn[g] > 0, jnp.maximum(t - nf, 0), ncol - 1)

    grid_spec = pltpu.PrefetchScalarGridSpec(
        num_scalar_prefetch=3,
        grid=(n_groups, nf + ncol),
        in_specs=[
            pl.BlockSpec(memory_space=pl.ANY),
            pl.BlockSpec((1, d, fch), lambda g, t, ge, gsb, gn: (ge[g], 0, gu_idx(g, t, ge, gsb, gn))),
            pl.BlockSpec((1, 1, fch), lambda g, t, ge, gsb, gn: (ge[g], 0, gu_idx(g, t, ge, gsb, gn))),
            pl.BlockSpec((1, d, fch), lambda g, t, ge, gsb, gn: (ge[g], 0, gu_idx(g, t, ge, gsb, gn))),
            pl.BlockSpec((1, 1, fch), lambda g, t, ge, gsb, gn: (ge[g], 0, gu_idx(g, t, ge, gsb, gn))),
            pl.BlockSpec((1, dff, fcn), lambda g, t, ge, gsb, gn: (ge[g], 0, dn_idx(g, t, ge, gsb, gn))),
            pl.BlockSpec((1, 1, fcn), lambda g, t, ge, gsb, gn: (ge[g], 0, dn_idx(g, t, ge, gsb, gn))),
        ],
        out_specs=pl.BlockSpec(memory_space=pl.ANY),
        scratch_shapes=[
            pltpu.VMEM((2, MOE_SB, d), jnp.float32),
            pltpu.VMEM((rg, d), jnp.bfloat16),
            pltpu.VMEM((nf, rg, fch), jnp.bfloat16),
            pltpu.VMEM((2, rg, fcn), jnp.float32),
            pltpu.VMEM((d, fch), jnp.bfloat16),
            pltpu.VMEM((d, fch), jnp.bfloat16),
            pltpu.VMEM((dff, fcn), jnp.bfloat16),
            pltpu.SemaphoreType.DMA((2,)),
            pltpu.SemaphoreType.DMA((2,)),
        ],
    )
    return pl.pallas_call(
        functools.partial(_moe_kernel, sb=MOE_SB, nsub=MOE_NSUB, nf=nf, ncol=ncol),
        grid_spec=grid_spec,
        out_shape=jax.ShapeDtypeStruct((n_rows, d), jnp.float32),
        compiler_params=_cparams(("arbitrary", "arbitrary")),
        name="moe_experts",
    )(g_expert, g_sb0, g_nsb, xr, w_gate, b_gate.reshape(ne, 1, dff), w_up, b_up.reshape(ne, 1, dff),
      w_down, b_down.reshape(ne, 1, d))


def _combine_kernel(dest_ref, h_ref, gate_ref, g_ref, b_ref, y_hbm, o_ref, gbuf, sem, *, td, alpha):
    i = pl.program_id(0)
    n = pl.num_programs(0)
    slot = i % 2

    def row_copy(step, t, k, sl):
        d = dest_ref[(step * td + t) * TOP_K + k]
        return pltpu.make_async_copy(y_hbm.at[pl.ds(d, 1)], gbuf.at[sl, k, pl.ds(t, 1)], sem.at[sl])

    def issue(step, sl):
        def body(t, carry):
            for k in range(TOP_K):
                row_copy(step, t, k, sl).start()
            return carry
        lax.fori_loop(0, td, body, 0)

    @pl.when(i == 0)
    def _():
        issue(0, 0)

    @pl.when(i + 1 < n)
    def _():
        issue(i + 1, 1 - slot)

    def wait(t, carry):
        for k in range(TOP_K):
            pltpu.make_async_copy(y_hbm.at[pl.ds(0, 1)], gbuf.at[slot, k, pl.ds(0, 1)], sem.at[slot]).wait()
        return carry
    lax.fori_loop(0, td, wait, 0)

    gates = gate_ref[...]
    moe = gates[:, 0:1] * gbuf[slot, 0]
    for k in range(1, TOP_K):
        moe = moe + gates[:, k:k + 1] * gbuf[slot, k]
    o_ref[...] = _layer_norm(alpha * h_ref[...] + moe, g_ref[...], b_ref[...])


def _combine(dest_flat, h, gates, ln_g, ln_b, yr, alpha):
    rows, d = h.shape
    td = _tile(rows, DISPATCH_TD)
    grid_spec = pltpu.PrefetchScalarGridSpec(
        num_scalar_prefetch=1,
        grid=(rows // td,),
        in_specs=[
            pl.BlockSpec((td, d), lambda i, dest: (i, 0)),
            pl.BlockSpec((td, TOP_K), lambda i, dest: (i, 0)),
            pl.BlockSpec((1, d), lambda i, dest: (0, 0)),
            pl.BlockSpec((1, d), lambda i, dest: (0, 0)),
            pl.BlockSpec(memory_space=pl.ANY),
        ],
        out_specs=pl.BlockSpec((td, d), lambda i, dest: (i, 0)),
        scratch_shapes=[pltpu.VMEM((2, TOP_K, td, d), jnp.float32), pltpu.SemaphoreType.DMA((2,))],
    )
    return pl.pallas_call(
        functools.partial(_combine_kernel, td=td, alpha=alpha),
        grid_spec=grid_spec,
        out_shape=jax.ShapeDtypeStruct((rows, d), jnp.float32),
        compiler_params=_cparams(("arbitrary",)),
        name="combine",
    )(dest_flat, h, gates, ln_g, ln_b, yr)


def _rope_tables(first, count):
    half = QK_DIM // 2
    log_theta = jnp.log(jnp.asarray(ROPE_THETA, jnp.float32))
    inv = jnp.exp(-log_theta * jnp.arange(half, dtype=jnp.float32) * (2.0 / QK_DIM))
    pos = first + jnp.arange(count)
    ang = pos.astype(jnp.float32)[:, None] * inv[None, :]
    cos, sin = jnp.cos(ang), jnp.sin(ang)
    reps = LANES // QK_DIM
    return jnp.tile(jnp.concatenate([cos, cos], axis=-1), (1, reps)), \
        jnp.tile(jnp.concatenate([-sin, sin], axis=-1), (1, reps))


def _shifted_taps(u, state):
    b, s, c = u.shape
    full = jnp.concatenate([state, u], axis=1)
    taps = [full[:, j:j + s].reshape(b * s, c) for j in range(CONV_K)]
    return taps, full[:, s:]


def _routing_tables(counts, n_groups):
    nsb = (counts + MOE_SB - 1) // MOE_SB
    sb_end = jnp.cumsum(nsb)
    sb_off = sb_end - nsb
    ng = (nsb + MOE_NSUB - 1) // MOE_NSUB
    g_end = jnp.cumsum(ng)
    g_off = g_end - ng
    gi = jnp.arange(n_groups, dtype=jnp.int32)
    last = jnp.maximum(g_end[-1] - 1, 0)
    ge = jnp.sum((g_end[None, :] <= jnp.minimum(gi, last)[:, None]).astype(jnp.int32), axis=1)
    ge = jnp.minimum(ge, counts.shape[0] - 1)
    local = gi - g_off[ge]
    g_sb0 = sb_off[ge] + local * MOE_NSUB
    g_nsb = jnp.where(gi < g_end[-1], jnp.clip(nsb[ge] - local * MOE_NSUB, 0, MOE_NSUB), 0)
    return sb_off * MOE_SB, ge, g_sb0.astype(jnp.int32), g_nsb.astype(jnp.int32)


def kernel(x_prompt, x_sample, cache_k, cache_v, state_conv, page_table, w_in, conv_w, lambda_q, lambda_k, subln_g, w_out, ln1_g, ln1_b, w_router, b_router, w_gate, b_gate, w_up, b_up, w_down, b_down, ln2_g, ln2_b):
    bp, sp, d = x_prompt.shape
    bs, ts, _ = x_sample.shape
    depth, n_pool = cache_k.shape[0], cache_k.shape[1]
    nh = cache_k.shape[3]
    att_dim = nh * V_DIM
    conv_dim = conv_w.shape[-1]
    ne = w_router.shape[-1]
    n_past = page_table.shape[1] * PAGE_SIZE
    alpha = (2.0 * depth) ** 0.25
    assert ts == 4 and cache_k.shape[2] == PAGE_SIZE and w_in.shape[-1] == 3 * att_dim + 3 * conv_dim

    cos_p, sin_p = _rope_tables(0, sp)
    cos_s, sin_s = _rope_tables(n_past, ts)
    cos_s, sin_s = jnp.tile(cos_s, (bs, 1)), jnp.tile(sin_s, (bs, 1))
    cache_k2 = cache_k.reshape(depth * n_pool, PAGE_SIZE * nh, V_DIM)
    cache_v2 = cache_v.reshape(depth * n_pool, PAGE_SIZE * nh, V_DIM)

    n_tok = bp * sp + bs * ts
    n_rows = (n_tok * TOP_K // MOE_SB + ne) * MOE_SB
    n_groups = ne + max(n_rows // MOE_SB - ne, 0) // MOE_NSUB + 1

    xp = x_prompt.reshape(bp * sp, d)
    xs = x_sample.reshape(bs * ts, d)
    outs = {k: [] for k in ('kp', 'vp', 'cp', 'ks', 'vs', 'cs')}
    for l in range(depth):
        lam_init = 0.8 - 0.6 * math.exp(-0.3 * l)
        w_in_bf = w_in[l].astype(jnp.bfloat16)
        w_out_bf = w_out[l].astype(jnp.bfloat16)
        lq, lk, sg = lambda_q[l], lambda_k[l], subln_g[l].reshape(1, V_DIM)
        g1, b1 = ln1_g[l].reshape(1, d), ln1_b[l].reshape(1, d)
        g2, b2 = ln2_g[l].reshape(1, d), ln2_b[l].reshape(1, d)
        br = b_router[l].reshape(1, ne)

        xp_bf = xp.astype(jnp.bfloat16)
        q_bf, k_f, k_bf, v_f, _, vt_bf = _qkv_proj(xp_bf, w_in_bf, cos_p, sin_p, att_dim)
        u_p, gb_p = _conv_proj(xp_bf, w_in_bf, att_dim, conv_dim)
        att_p = _flash_prompt(q_bf, k_bf, vt_bf, lq, lk, sg.reshape(V_DIM, 1), bp, sp, lam_init)
        conv_p = _conv_gate_halo(u_p, gb_p, conv_w[l], sp)
        cst_p = u_p.reshape(bp, sp, conv_dim)[:, sp - (CONV_K - 1):]
        outs['kp'].append(k_f.reshape(bp, sp, nh, V_DIM))
        outs['vp'].append(v_f.reshape(bp, sp, nh, V_DIM))
        outs['cp'].append(cst_p)

        xs_bf = xs.astype(jnp.bfloat16)
        qs_bf, ks_f, ks_bf, vs_f, vs_bf, _ = _qkv_proj(xs_bf, w_in_bf, cos_s, sin_s, att_dim)
        u_s, gb_s = _conv_proj(xs_bf, w_in_bf, att_dim, conv_dim)
        q5 = qs_bf.reshape(bs, ts, nh, 2, QK_DIM)
        eye = jnp.eye(2, dtype=q5.dtype)
        qall = jnp.einsum('bthmd,mn->bhmtnd', q5, eye).reshape(bs, nh * 2 * ts, V_DIM)
        att_s = _sample_attention(qall, ks_bf.reshape(bs, ts * nh, V_DIM), vs_bf.reshape(bs, ts * nh, V_DIM),
                                  cache_k2, cache_v2, page_table, l, n_pool, lq, lk, sg, ts, lam_init)
        att_s = att_s.reshape(bs, nh, 2 * ts, V_DIM)[:, :, :ts]
        att_s = jnp.transpose(att_s, (0, 2, 1, 3)).reshape(bs * ts, att_dim).astype(jnp.bfloat16)
        taps, cst_s = _shifted_taps(u_s.reshape(bs, ts, conv_dim), state_conv[l])
        conv_s = _conv_gate(*taps, gb_s, conv_w[l])
        outs['ks'].append(ks_f.reshape(bs, ts, nh, V_DIM))
        outs['vs'].append(vs_f.reshape(bs, ts, nh, V_DIM))
        outs['cs'].append(cst_s)

        cnt0 = jnp.zeros((1, ne), jnp.float32)
        h_p, ids_p, rank_p, gate_p, cnt_p = _out_proj_route(att_p, conv_p, xp, w_out_bf, g1, b1, w_router[l], br, cnt0, alpha)
        h_s, ids_s, rank_s, gate_s, cnt_all = _out_proj_route(att_s, conv_s, xs, w_out_bf, g1, b1, w_router[l], br, cnt_p, alpha)

        counts = cnt_all.reshape(ne).astype(jnp.int32)
        row_off, g_expert, g_sb0, g_nsb = _routing_tables(counts, n_groups)
        dest_p = (row_off[ids_p.reshape(-1)] + rank_p.reshape(-1)).astype(jnp.int32)
        dest_s = (row_off[ids_s.reshape(-1)] + rank_s.reshape(-1)).astype(jnp.int32)

        xr = _dispatch(dest_p, h_p, None, n_rows)
        xr = _dispatch(dest_s, h_s, xr, n_rows)
        yr = _moe_experts(g_expert, g_sb0, g_nsb, xr, w_gate[l], b_gate[l], w_up[l], b_up[l], w_down[l], b_down[l])
        xp = _combine(dest_p, h_p, gate_p, g2, b2, yr, alpha)
        xs = _combine(dest_s, h_s, gate_s, g2, b2, yr, alpha)

    return (xp.reshape(bp, sp, d), xs.reshape(bs, ts, d),
            jnp.stack(outs['kp']), jnp.stack(outs['vp']), jnp.stack(outs['cp']),
            jnp.stack(outs['ks']), jnp.stack(outs['vs']), jnp.stack(outs['cs']))
```
